```python
import jax, jax.numpy as jnp
from jax import lax
import numpy as np

D_MODEL = 2048
BATCH = 4
SEQ = 4096
DEPTH = 2

N_A_LAYERS = DEPTH // 2
N_B_LAYERS = DEPTH - N_A_LAYERS
CONV_WIDTH = 31
N_HEADS = 16
HEAD_DIM = D_MODEL // N_HEADS
D_FF = -(-(8 * D_MODEL) // (3 * 256)) * 256
BRANCHES = ((128, 1), (512, 4), (2048, 16))
BLOCK = 128
RMS_EPS = 1e-6
LN_EPS = 1e-5

kernel_name = "yoco_conformer_dilated_alibi"


def _rmsnorm(x, g):
    xf = x.astype(jnp.float32)
    y = xf * lax.rsqrt(jnp.mean(xf * xf, axis=-1, keepdims=True) + RMS_EPS)
    return (y * g.astype(jnp.float32)).astype(x.dtype)


def _layernorm(x, g, b):
    xf = x.astype(jnp.float32)
    mu = jnp.mean(xf, axis=-1, keepdims=True)
    var = jnp.mean(jnp.square(xf - mu), axis=-1, keepdims=True)
    y = (xf - mu) * lax.rsqrt(var + LN_EPS) * g.astype(jnp.float32) + b.astype(jnp.float32)
    return y.astype(x.dtype)


def _swiglu(h, w_gate, w_up, w_down):
    return (jax.nn.silu(h @ w_gate) * (h @ w_up)) @ w_down


def _conformer_conv(h, w1, b1, dw, dw_b, ln_g, ln_b, w2, b2):
    u = h @ w1 + b1
    a, gate = jnp.split(u, 2, axis=-1)
    u = a * jax.nn.sigmoid(gate)
    u = lax.conv_general_dilated(
        u, dw[:, None, :].astype(u.dtype), window_strides=(1,),
        padding=[(CONV_WIDTH - 1, 0)],
        dimension_numbers=("NWC", "WIO", "NWC"),
        feature_group_count=u.shape[-1]) + dw_b
    u = jax.nn.silu(_layernorm(u, ln_g, ln_b))
    return u @ w2 + b2


def _alibi_slopes():
    h = jnp.arange(1, N_HEADS + 1, dtype=jnp.float32)
    return jnp.exp2(-8.0 * h / N_HEADS)


def _residue_major(t, d):
    b, s, h, e = t.shape
    L = s // d
    return t.reshape(b, L, d, h, e).transpose(0, 2, 1, 3, 4), L


def _shared_branch_kv(k, v):
    out = []
    for window, d in BRANCHES:
        n_off = window // d
        kr, L = _residue_major(k, d)
        vr, _ = _residue_major(v, d)
        nb = -(-L // BLOCK)
        pad = ((0, 0), (0, 0), (n_off, nb * BLOCK - L), (0, 0), (0, 0))
        idx = jnp.arange(nb)[:, None] * BLOCK + jnp.arange(BLOCK + n_off)[None, :]
        out.append((jnp.pad(kr, pad)[:, :, idx], jnp.pad(vr, pad)[:, :, idx]))
    return out


def _dilated_attention(q, branch_kv):
    b, s, h, e = q.shape
    slopes = _alibi_slopes()
    scale = HEAD_DIM ** -0.5
    outs, lses = [], []
    for (window, d), (k_win, v_win) in zip(BRANCHES, branch_kv):
        n_off = window // d
        qr, L = _residue_major(q, d)
        nb = -(-L // BLOCK)
        qr = jnp.pad(qr, ((0, 0), (0, 0), (0, nb * BLOCK - L), (0, 0), (0, 0)))
        qr = qr.reshape(b, d, nb, BLOCK, h, e)
        scores = jnp.einsum("brnqhe,brnkhe->brnhqk", qr, k_win,
                            preferred_element_type=jnp.float32) * scale
        qi = jnp.arange(BLOCK)[:, None]
        kk = jnp.arange(BLOCK + n_off)[None, :]
        j = qi - kk + n_off
        key_idx = jnp.arange(nb)[:, None, None] * BLOCK + kk[None] - n_off
        valid = (j >= 0) & (j <= n_off) & (key_idx >= 0)
        bias = -slopes[:, None, None] * (d * j).astype(jnp.float32)
        logits = jnp.where(valid[None, None, :, None], scores + bias[None, None, None], -jnp.inf)
        m = jnp.max(logits, axis=-1, keepdims=True)
        p = jnp.exp(logits - m)
        den = jnp.sum(p, axis=-1, keepdims=True)
        o = jnp.einsum("brnhqk,brnkhe->brnqhe", p / den, v_win.astype(jnp.float32))
        lse = (m + jnp.log(den))[..., 0].transpose(0, 1, 2, 4, 3)
        o = o.reshape(b, d, nb * BLOCK, h, e)[:, :, :L].transpose(0, 2, 1, 3, 4).reshape(b, s, h, e)
        lse = lse.reshape(b, d, nb * BLOCK, h)[:, :, :L].transpose(0, 2, 1, 3).reshape(b, s, h)
        outs.append(o)
        lses.append(lse)
    w = jax.nn.softmax(jnp.stack(lses, axis=0), axis=0)
    return jnp.sum(w[..., None] * jnp.stack(outs, axis=0), axis=0)


def setup_inputs(seed: int = 0) -> dict:
    key = jax.random.key(seed)
    ks = jax.random.split(key, 24)
    D, F, W = D_MODEL, D_FF, CONV_WIDTH

    def dense(k, shape, fan_in):
        return jax.random.normal(k, shape, jnp.float32) * (fan_in ** -0.5)

    def gain(k, shape):
        return 1.0 + 0.02 * jax.random.normal(k, shape, jnp.float32)

    def bias(k, shape):
        return 0.02 * jax.random.normal(k, shape, jnp.float32)

    return {
        "x": jax.random.normal(ks[0], (BATCH, SEQ, D), jnp.float32),
        "a_norm_g": gain(ks[1], (N_A_LAYERS, D)),
        "conv_w1": dense(ks[2], (N_A_LAYERS, D, 2 * D), D),
        "conv_b1": bias(ks[3], (N_A_LAYERS, 2 * D)),
        "conv_dw": dense(ks[4], (N_A_LAYERS, W, D), W),
        "conv_dw_b": bias(ks[5], (N_A_LAYERS, D)),
        "conv_ln_g": gain(ks[6], (N_A_LAYERS, D)),
        "conv_ln_b": bias(ks[7], (N_A_LAYERS, D)),
        "conv_w2": dense(ks[8], (N_A_LAYERS, D, D), D),
        "conv_b2": bias(ks[9], (N_A_LAYERS, D)),
        "kv_norm_g": gain(ks[10], (D,)),
        "w_k": dense(ks[11], (D, D), D),
        "w_v": dense(ks[12], (D, D), D),
        "b_norm_g": gain(ks[13], (N_B_LAYERS, D)),
        "w_q": dense(ks[14], (N_B_LAYERS, D, D), D),
        "w_o": dense(ks[15], (N_B_LAYERS, D, D), D),
        "ffn_norm_g": gain(ks[16], (DEPTH, D)),
        "ffn_w_gate": dense(ks[17], (DEPTH, D, F), D),
        "ffn_w_up": dense(ks[18], (DEPTH, D, F), D),
        "ffn_w_down": dense(ks[19], (DEPTH, F, D), F),
        "final_norm_g": gain(ks[20], (D,)),
    }


def reference(x, a_norm_g, conv_w1, conv_b1, conv_dw, conv_dw_b, conv_ln_g, conv_ln_b,
              conv_w2, conv_b2, kv_norm_g, w_k, w_v, b_norm_g, w_q, w_o,
              ffn_norm_g, ffn_w_gate, ffn_w_up, ffn_w_down, final_norm_g):
    b, s, _ = x.shape
    h = x
    shared_kv = None
    for layer in range(DEPTH):
        if layer < N_A_LAYERS:
            a = layer
            h = h + _conformer_conv(_rmsnorm(h, a_norm_g[a]), conv_w1[a], conv_b1[a],
                                    conv_dw[a], conv_dw_b[a], conv_ln_g[a], conv_ln_b[a],
                                    conv_w2[a], conv_b2[a])
        else:
            if layer == N_A_LAYERS:
                kv_in = _rmsnorm(h, kv_norm_g)
                k = (kv_in @ w_k).reshape(b, s, N_HEADS, HEAD_DIM)
                v = (kv_in @ w_v).reshape(b, s, N_HEADS, HEAD_DIM)
                shared_kv = _shared_branch_kv(k, v)
            i = layer - N_A_LAYERS
            q = (_rmsnorm(h, b_norm_g[i]) @ w_q[i]).reshape(b, s, N_HEADS, HEAD_DIM)
            att = _dilated_attention(q, shared_kv).astype(h.dtype).reshape(b, s, D_MODEL)
            h = h + att @ w_o[i]
        h = h + _swiglu(_rmsnorm(h, ffn_norm_g[layer]), ffn_w_gate[layer],
                        ffn_w_up[layer], ffn_w_down[layer])
    return _rmsnorm(h, final_norm_g)
```

```python
import functools

import numpy as np
import jax
import jax.numpy as jnp
from jax import lax
from jax.experimental import pallas as pl
from jax.experimental.pallas import tpu as pltpu

RMS_EPS = 1e-6
LN_EPS = 1e-5
N_HEADS = 16
CONV_WIDTH = 31
BRANCH_DILATIONS = (1, 4, 16)
WINDOW_STEPS = 128
Q_BLOCK = 128
RESIDUES = 16
MASKED = 1e30

_VMEM_LIMIT = 56 * 1024 * 1024
_BF16 = jnp.bfloat16
_F32 = jnp.float32


def _params(n_axes):
    return pltpu.CompilerParams(dimension_semantics=("arbitrary",) * n_axes,
                                vmem_limit_bytes=_VMEM_LIMIT)


def _sigmoid(x):
    return 1.0 / (1.0 + jnp.exp(-x))


def _rmsnorm_rows(x_ref, out_refs, g_refs, chunk=32):
    n = x_ref.shape[0] // chunk

    def body(i, carry):
        r = pl.multiple_of(i * chunk, chunk)
        x = x_ref[pl.ds(r, chunk), :]
        xhat = x * lax.rsqrt(jnp.mean(x * x, axis=-1, keepdims=True) + RMS_EPS)
        for o_ref, g_ref in zip(out_refs, g_refs):
            o_ref[pl.ds(r, chunk), :] = (xhat * g_ref[...]).astype(o_ref.dtype)
        return carry

    lax.fori_loop(0, n, body, 0)


def _glu_in_kernel(x_ref, g_ref, wa_ref, wg_ref, ba_ref, bg_ref, u_ref, xn_ref):
    @pl.when(pl.program_id(1) == 0)
    def _():
        _rmsnorm_rows(x_ref, [xn_ref], [g_ref])

    xn = xn_ref[...]
    a = jnp.dot(xn, wa_ref[...], preferred_element_type=_F32) + ba_ref[...]
    gate = jnp.dot(xn, wg_ref[...], preferred_element_type=_F32) + bg_ref[...]
    u_ref[...] = a * _sigmoid(gate)


def _glu_in(x, g, w1, b1, *, tm=1024, tn=512):
    n, d = x.shape
    nj = d // tn
    return pl.pallas_call(
        _glu_in_kernel,
        grid=(n // tm, nj),
        in_specs=[
            pl.BlockSpec((tm, d), lambda i, j: (i, 0)),
            pl.BlockSpec((1, d), lambda i, j: (0, 0)),
            pl.BlockSpec((d, tn), lambda i, j: (0, j)),
            pl.BlockSpec((d, tn), lambda i, j: (0, j + nj)),
            pl.BlockSpec((1, tn), lambda i, j: (0, j)),
            pl.BlockSpec((1, tn), lambda i, j: (0, j + nj)),
        ],
        out_specs=pl.BlockSpec((tm, tn), lambda i, j: (i, j)),
        out_shape=jax.ShapeDtypeStruct((n, d), _F32),
        scratch_shapes=[pltpu.VMEM((tm, d), _BF16)],
        compiler_params=_params(2),
        name="glu_in",
    )(x, g, w1, w1, b1, b1)


_HALO = 32


def _conv_out_kernel(u_ref, halo_ref, x_ref, dw_ref, dwb_ref, lng_ref, lnb_ref, w2_ref, b2_ref,
                     o_ref, buf_ref, y_ref, z_ref, *, blocks_per_seq):
    tm, d = u_ref.shape
    first = (pl.program_id(0) % blocks_per_seq) == 0
    buf_ref[0:_HALO, :] = jnp.where(first, 0.0, halo_ref[...])
    buf_ref[_HALO:, :] = u_ref[...]

    def conv_lanes(c, carry):
        lanes = pl.ds(pl.multiple_of(c * 128, 128), 128)
        w = dw_ref[:, lanes]
        acc = jnp.broadcast_to(dwb_ref[:, lanes], (tm, 128))
        for k in range(CONV_WIDTH):
            acc = acc + buf_ref[pl.ds(_HALO - (CONV_WIDTH - 1) + k, tm), lanes] * w[k:k + 1, :]
        y_ref[:, lanes] = acc
        return carry

    lax.fori_loop(0, d // 128, conv_lanes, 0)

    chunk = 32

    def norm_rows(i, carry):
        rows = pl.ds(pl.multiple_of(i * chunk, chunk), chunk)
        y = y_ref[rows, :]
        mu = jnp.mean(y, axis=-1, keepdims=True)
        yc = y - mu
        var = jnp.mean(yc * yc, axis=-1, keepdims=True)
        zn = yc * lax.rsqrt(var + LN_EPS) * lng_ref[...] + lnb_ref[...]
        z_ref[rows, :] = (zn * _sigmoid(zn)).astype(z_ref.dtype)
        return carry

    lax.fori_loop(0, tm // chunk, norm_rows, 0)

    o_ref[...] = (x_ref[...] + b2_ref[...]
                  + jnp.dot(z_ref[...], w2_ref[...], preferred_element_type=_F32))


def _conv_out(u, x, dw, dwb, lng, lnb, w2, b2, *, seq, tm=256):
    n, d = u.shape
    per_halo = tm // _HALO
    kern = functools.partial(_conv_out_kernel, blocks_per_seq=seq // tm)
    row = lambda i: (0, 0)
    return pl.pallas_call(
        kern,
        grid=(n // tm,),
        in_specs=[
            pl.BlockSpec((tm, d), lambda i: (i, 0)),
            pl.BlockSpec((_HALO, d), lambda i: (jnp.maximum(i * per_halo - 1, 0), 0)),
            pl.BlockSpec((tm, d), lambda i: (i, 0)),
            pl.BlockSpec((CONV_WIDTH, d), row),
            pl.BlockSpec((1, d), row),
            pl.BlockSpec((1, d), row),
            pl.BlockSpec((1, d), row),
            pl.BlockSpec((d, d), row),
            pl.BlockSpec((1, d), row),
        ],
        out_specs=pl.BlockSpec((tm, d), lambda i: (i, 0)),
        out_shape=jax.ShapeDtypeStruct((n, d), _F32),
        scratch_shapes=[pltpu.VMEM((tm + _HALO, d), _F32),
                        pltpu.VMEM((tm, d), _F32),
                        pltpu.VMEM((tm, d), _BF16)],
        compiler_params=_params(1),
        name="conv_out",
    )(u, u, x, dw, dwb, lng, lnb, w2, b2)


def _ffn_kernel(x_ref, g_ref, wg_ref, wu_ref, wd_ref, *rest, final_norm):
    if final_norm:
        gf_ref, o_ref, xn_ref, acc_ref = rest
    else:
        o_ref, xn_ref, acc_ref = rest
    f = pl.program_id(1)

    @pl.when(f == 0)
    def _():
        _rmsnorm_rows(x_ref, [xn_ref], [g_ref])
        acc_ref[...] = x_ref[...]

    xn = xn_ref[...]
    gate = jnp.dot(xn, wg_ref[...], preferred_element_type=_F32)
    up = jnp.dot(xn, wu_ref[...], preferred_element_type=_F32)
    act = (gate * _sigmoid(gate) * up).astype(_BF16)
    acc_ref[...] += jnp.dot(act, wd_ref[...], preferred_element_type=_F32)

    @pl.when(f == pl.num_programs(1) - 1)
    def _():
        if final_norm:
            _rmsnorm_rows(acc_ref, [o_ref], [gf_ref])
        else:
            o_ref[...] = acc_ref[...]


def _ffn(x, g, wg, wu, wd, gf=None, *, tm=512, tf=512):
    n, d = x.shape
    f = wg.shape[1]
    final_norm = gf is not None
    in_specs = [
        pl.BlockSpec((tm, d), lambda i, j: (i, 0)),
        pl.BlockSpec((1, d), lambda i, j: (0, 0)),
        pl.BlockSpec((d, tf), lambda i, j: (0, j)),
        pl.BlockSpec((d, tf), lambda i, j: (0, j)),
        pl.BlockSpec((tf, d), lambda i, j: (j, 0)),
    ]
    args = [x, g, wg, wu, wd]
    if final_norm:
        in_specs.append(pl.BlockSpec((1, d), lambda i, j: (0, 0)))
        args.append(gf)
    return pl.pallas_call(
        functools.partial(_ffn_kernel, final_norm=final_norm),
        grid=(n // tm, f // tf),
        in_specs=in_specs,
        out_specs=pl.BlockSpec((tm, d), lambda i, j: (i, 0)),
        out_shape=jax.ShapeDtypeStruct((n, d), _F32),
        scratch_shapes=[pltpu.VMEM((tm, d), _BF16), pltpu.VMEM((tm, d), _F32)],
        compiler_params=_params(2),
        name="ffn_final" if final_norm else "ffn",
    )(*args)


def _kvq_kernel(x_ref, gkv_ref, gq_ref, w_ref, o_ref, xkv_ref, xq_ref, *, q_scale):
    _rmsnorm_rows(x_ref, [xkv_ref, xq_ref], [gkv_ref, gq_ref])
    xkv = xkv_ref[...]
    o_ref[0] = jnp.dot(xkv, w_ref[0], preferred_element_type=_F32)
    o_ref[1] = jnp.dot(xkv, w_ref[1], preferred_element_type=_F32)
    o_ref[2] = jnp.dot(xq_ref[...], w_ref[2], preferred_element_type=_F32) * q_scale


def _kvq(x, gkv, gq, w_kvq, *, batch, seq, q_scale):
    d = x.shape[1]
    per = seq // RESIDUES
    x3 = x.reshape(batch, per, RESIDUES * d)
    return pl.pallas_call(
        functools.partial(_kvq_kernel, q_scale=q_scale),
        grid=(batch * RESIDUES,),
        in_specs=[
            pl.BlockSpec((None, per, d), lambda s: (s // RESIDUES, 0, s % RESIDUES)),
            pl.BlockSpec((1, d), lambda s: (0, 0)),
            pl.BlockSpec((1, d), lambda s: (0, 0)),
            pl.BlockSpec((3, d, d), lambda s: (0, 0, 0), pipeline_mode=pl.Buffered(1)),
        ],
        out_specs=pl.BlockSpec((3, None, per, d), lambda s: (0, s // RESIDUES, s % RESIDUES, 0)),
        out_shape=jax.ShapeDtypeStruct((3, batch, seq, d), _F32),
        scratch_shapes=[pltpu.VMEM((per, d), _BF16), pltpu.VMEM((per, d), _BF16)],
        compiler_params=_params(1),
        name="kvq",
    )(x3, gkv, gq, w_kvq)


def _branch_tables():
    tables = []
    for dil in BRANCH_DILATIONS:
        c = RESIDUES // dil
        p = Q_BLOCK // c
        kq, pq = np.divmod(np.arange(Q_BLOCK), p)
        kk, ck = np.divmod(np.arange(2 * Q_BLOCK), 2 * p)
        pair = []
        for shift in (0, p):
            j = c * (shift + pq[:, None] - ck[None, :]) + (kq[:, None] - kk[None, :])
            valid = (j >= 0) & (j <= WINDOW_STEPS)
            pair.append(np.where(valid, (dil * j).astype(np.float32), np.float32(MASKED)))
        tables.append(np.stack(pair))
    return np.stack(tables).astype(np.float32)


def _attention_kernel(slopes_ref, tab_ref, q_ref, k_ref, v_ref, o_ref,
                      bias_ref, acc_ref, m_ref, l_ref):
    seq = q_ref.shape[0]
    per = seq // RESIDUES
    h = pl.program_id(1)
    neg_slope = -slopes_ref[h]
    n_br = len(BRANCH_DILATIONS)
    for b in range(n_br):
        for t in range(2):
            bias_ref[b, t] = tab_ref[b, t] * neg_slope

    def gather(ref, starts, rows):
        return jnp.concatenate([ref[pl.ds(s, rows), :] for s in starts], axis=0).astype(_BF16)

    def block(b, chunks, q_off, k_off, first):
        p = Q_BLOCK // len(chunks)
        aligned = lambda r: r if isinstance(r, int) else pl.multiple_of(r, 8)
        q_starts = [aligned(ch * per + q_off) for ch in chunks]
        k_starts = [aligned(ch * per + k_off) for ch in chunks]
        q = gather(q_ref, q_starts, p)
        k = gather(k_ref, k_starts, 2 * p)
        v = gather(v_ref, k_starts, 2 * p)
        s = lax.dot_general(q, k, (((1,), (1,)), ((), ())), preferred_element_type=_F32)
        s = s + bias_ref[b, 0 if first else 1]
        m = jnp.max(s, axis=-1, keepdims=True)
        e = jnp.exp(s - m)
        l = jnp.sum(e, axis=-1, keepdims=True)
        acc = jnp.dot(e.astype(_BF16), v, preferred_element_type=_F32)
        mb = jnp.broadcast_to(m, (Q_BLOCK, 128))
        lb = jnp.broadcast_to(l, (Q_BLOCK, 128))
        for i, s0 in enumerate(q_starts):
            rows = pl.ds(s0, p)
            acc_ref[b, rows, :] = acc[i * p:(i + 1) * p]
            m_ref[b, rows, :] = mb[i * p:(i + 1) * p]
            l_ref[b, rows, :] = lb[i * p:(i + 1) * p]

    for b, dil in enumerate(BRANCH_DILATIONS):
        c = RESIDUES // dil
        p = Q_BLOCK // c
        n_blocks = per // p

        def residue(res, b=b, dil=dil, c=c, p=p, n_blocks=n_blocks):
            chunks = [res + dil * kk for kk in range(c)]
            block(b, chunks, 0, 0, True)

            def body(nb, carry):
                off = pl.multiple_of(nb * p, 8)
                block(b, chunks, off, off - p, False)
                return carry

            lax.fori_loop(1, n_blocks, body, 0)

        if dil == RESIDUES:
            lax.fori_loop(0, dil, lambda res, carry: (residue(res), carry)[1], 0)
        else:
            for res in range(dil):
                residue(res)

    rows_per = 64

    def merge(i, carry):
        rows = pl.ds(pl.multiple_of(i * rows_per, rows_per), rows_per)
        ms = [m_ref[b, rows, :] for b in range(n_br)]
        m_all = functools.reduce(jnp.maximum, ms)
        num = 0.0
        den = 0.0
        for b in range(n_br):
            w = jnp.exp(ms[b] - m_all)
            num = num + w * acc_ref[b, rows, :]
            den = den + w * l_ref[b, rows, :]
        res = i // (per // rows_per)
        i0 = (i % (per // rows_per)) * rows_per
        o_ref[pl.ds(i0 * RESIDUES + res, rows_per, stride=RESIDUES), :] = num / den
        return carry

    lax.fori_loop(0, seq // rows_per, merge, 0)


def _attention(kvq, slopes, tables):
    _, batch, seq, d = kvq.shape
    e = d // N_HEADS
    n_br = len(BRANCH_DILATIONS)

    def qkv_spec(which):
        return pl.BlockSpec((None, None, seq, e), lambda b, h: (which, b, 0, h))

    return pl.pallas_call(
        _attention_kernel,
        grid=(batch, N_HEADS),
        in_specs=[
            pl.BlockSpec(memory_space=pltpu.SMEM),
            pl.BlockSpec(tables.shape, lambda b, h: (0, 0, 0, 0)),
            qkv_spec(2), qkv_spec(0), qkv_spec(1),
        ],
        out_specs=pl.BlockSpec((None, seq, e), lambda b, h: (b, 0, h)),
        out_shape=jax.ShapeDtypeStruct((batch, seq, d), _F32),
        scratch_shapes=[
            pltpu.VMEM((n_br, 2, Q_BLOCK, 2 * Q_BLOCK), _F32),
            pltpu.VMEM((n_br, seq, e), _F32),
            pltpu.VMEM((n_br, seq, 128), _F32),
            pltpu.VMEM((n_br, seq, 128), _F32),
        ],
        compiler_params=_params(2),
        name="dilated_attention",
    )(slopes, tables, kvq, kvq, kvq)


def _out_proj_kernel(a_ref, x_ref, w_ref, o_ref):
    o_ref[...] = x_ref[...] + jnp.dot(a_ref[...].astype(_BF16), w_ref[...],
                                      preferred_element_type=_F32)


def _out_proj(att, x, w, *, tm=512):
    n, d = x.shape
    return pl.pallas_call(
        _out_proj_kernel,
        grid=(n // tm,),
        in_specs=[
            pl.BlockSpec((tm, d), lambda i: (i, 0)),
            pl.BlockSpec((tm, d), lambda i: (i, 0)),
            pl.BlockSpec((d, d), lambda i: (0, 0)),
        ],
        out_specs=pl.BlockSpec((tm, d), lambda i: (i, 0)),
        out_shape=jax.ShapeDtypeStruct((n, d), _F32),
        compiler_params=_params(1),
        name="out_proj",
    )(att, x, w)


def kernel(x, a_norm_g, conv_w1, conv_b1, conv_dw, conv_dw_b, conv_ln_g, conv_ln_b, conv_w2, conv_b2,
           kv_norm_g, w_k, w_v, b_norm_g, w_q, w_o, ffn_norm_g, ffn_w_gate, ffn_w_up, ffn_w_down,
           final_norm_g):
    batch, seq, d = x.shape
    n_a = a_norm_g.shape[0]
    n_b = b_norm_g.shape[0]
    depth = n_a + n_b
    assert d % N_HEADS == 0 and d // N_HEADS == 128
    assert seq % (RESIDUES * Q_BLOCK) == 0

    row = lambda v: v.reshape(1, -1)
    h = x.reshape(batch * seq, d)
    head_dim = d // N_HEADS
    slopes = jnp.exp2(-8.0 * jnp.arange(1, N_HEADS + 1, dtype=_F32) / N_HEADS)
    tables = jnp.asarray(_branch_tables())
    kvq = None

    for layer in range(depth):
        if layer < n_a:
            a = layer
            u = _glu_in(h, row(a_norm_g[a]), conv_w1[a].astype(_BF16), row(conv_b1[a]))
            h = _conv_out(u, h, conv_dw[a], row(conv_dw_b[a]), row(conv_ln_g[a]), row(conv_ln_b[a]),
                          conv_w2[a].astype(_BF16), row(conv_b2[a]), seq=seq)
        else:
            i = layer - n_a
            assert n_b == 1, "one attention layer supported"
            w_kvq = jnp.stack([w_k, w_v, w_q[i]]).astype(_BF16)
            kvq = _kvq(h, row(kv_norm_g), row(b_norm_g[i]), w_kvq,
                       batch=batch, seq=seq, q_scale=head_dim ** -0.5)
            att = _attention(kvq, slopes, tables)
            h = _out_proj(att.reshape(batch * seq, d), h, w_o[i].astype(_BF16))
        last = layer == depth - 1
        h = _ffn(h, row(ffn_norm_g[layer]), ffn_w_gate[layer].astype(_BF16),
                 ffn_w_up[layer].astype(_BF16), ffn_w_down[layer].astype(_BF16),
                 row(final_norm_g) if last else None)
    return h.reshape(batch, seq, d)
```

```python
import functools
import math

import numpy as np
import jax
import jax.numpy as jnp
from jax import lax
from jax.experimental import pallas as pl
from jax.experimental.pallas import tpu as pltpu

RMS_EPS = 1e-6
LN_EPS = 1e-5
N_HEADS = 16
CONV_WIDTH = 31
BRANCH_DILATIONS = (1, 4, 16)
WINDOW_STEPS = 128
Q_BLOCK = 128
BLOCKS_PER_STEP = 4
MASKED = 1e30
LOG2E = math.log2(math.e)
LANES = 128

_VMEM_LIMIT = 56 * 1024 * 1024
_BF16 = jnp.bfloat16
_F32 = jnp.float32


def _params(n_axes):
    return pltpu.CompilerParams(dimension_semantics=("arbitrary",) * n_axes,
                                vmem_limit_bytes=_VMEM_LIMIT)


def _sigmoid(x):
    return 1.0 / (1.0 + jnp.exp(-x))


def _rmsnorm_rows(x_ref, out_refs, g_refs, chunk=32):
    n = x_ref.shape[0] // chunk

    def body(i, carry):
        r = pl.multiple_of(i * chunk, chunk)
        x = x_ref[pl.ds(r, chunk), :]
        xhat = x * lax.rsqrt(jnp.mean(x * x, axis=-1, keepdims=True) + RMS_EPS)
        for o_ref, g_ref in zip(out_refs, g_refs):
            o_ref[pl.ds(r, chunk), :] = (xhat * g_ref[...]).astype(o_ref.dtype)
        return carry

    lax.fori_loop(0, n, body, 0)


def _glu_in_kernel(x_ref, g_ref, wa_ref, wg_ref, ba_ref, bg_ref, u_ref, xn_ref):
    @pl.when(pl.program_id(1) == 0)
    def _():
        _rmsnorm_rows(x_ref, [xn_ref], [g_ref])

    xn = xn_ref[...]
    a = jnp.dot(xn, wa_ref[...], preferred_element_type=_F32) + ba_ref[...]
    gate = jnp.dot(xn, wg_ref[...], preferred_element_type=_F32) + bg_ref[...]
    u_ref[...] = a * _sigmoid(gate)


def _glu_in(x, g, w1, b1, *, tm=1024, tn=512):
    n, d = x.shape
    nj = d // tn
    return pl.pallas_call(
        _glu_in_kernel,
        grid=(n // tm, nj),
        in_specs=[
            pl.BlockSpec((tm, d), lambda i, j: (i, 0)),
            pl.BlockSpec((1, d), lambda i, j: (0, 0)),
            pl.BlockSpec((d, tn), lambda i, j: (0, j)),
            pl.BlockSpec((d, tn), lambda i, j: (0, j + nj)),
            pl.BlockSpec((1, tn), lambda i, j: (0, j)),
            pl.BlockSpec((1, tn), lambda i, j: (0, j + nj)),
        ],
        out_specs=pl.BlockSpec((tm, tn), lambda i, j: (i, j)),
        out_shape=jax.ShapeDtypeStruct((n, d), _F32),
        scratch_shapes=[pltpu.VMEM((tm, d), _BF16)],
        compiler_params=_params(2),
        name="glu_in",
    )(x, g, w1, w1, b1, b1)


_HALO = 32


def _conv_out_kernel(u_ref, halo_ref, x_ref, dw_ref, dwb_ref, lng_ref, lnb_ref, w2_ref, b2_ref,
                     o_ref, buf_ref, y_ref, z_ref, *, blocks_per_seq):
    tm, d = u_ref.shape
    first = (pl.program_id(0) % blocks_per_seq) == 0
    for c in range(d // LANES):
        lanes = slice(c * LANES, (c + 1) * LANES)
        buf_ref[c, 0:_HALO, :] = jnp.where(first, 0.0, halo_ref[:, lanes])
        buf_ref[c, _HALO:, :] = u_ref[:, lanes]

    def conv_lanes(c, carry):
        lanes = pl.ds(pl.multiple_of(c * LANES, LANES), LANES)
        w = dw_ref[:, lanes]
        acc = jnp.broadcast_to(dwb_ref[:, lanes], (tm, LANES))
        for k in range(CONV_WIDTH):
            shifted = buf_ref[c, pl.ds(_HALO - (CONV_WIDTH - 1) + k, tm, stride=1), :]
            acc = acc + shifted * w[k:k + 1, :]
        y_ref[:, lanes] = acc
        return carry

    lax.fori_loop(0, d // LANES, conv_lanes, 0)

    chunk = 64

    def norm_rows(i, carry):
        rows = pl.ds(pl.multiple_of(i * chunk, chunk), chunk)
        y = y_ref[rows, :]
        mu = jnp.mean(y, axis=-1, keepdims=True)
        yc = y - mu
        var = jnp.mean(yc * yc, axis=-1, keepdims=True)
        zn = yc * lax.rsqrt(var + LN_EPS) * lng_ref[...] + lnb_ref[...]
        z_ref[rows, :] = (zn * _sigmoid(zn)).astype(z_ref.dtype)
        return carry

    lax.fori_loop(0, tm // chunk, norm_rows, 0)

    o_ref[...] = (x_ref[...] + b2_ref[...]
                  + jnp.dot(z_ref[...], w2_ref[...], preferred_element_type=_F32))


def _conv_out(u, x, dw, dwb, lng, lnb, w2, b2, *, seq, tm=256):
    n, d = u.shape
    per_halo = tm // _HALO
    kern = functools.partial(_conv_out_kernel, blocks_per_seq=seq // tm)
    row = lambda i: (0, 0)
    return pl.pallas_call(
        kern,
        grid=(n // tm,),
        in_specs=[
            pl.BlockSpec((tm, d), lambda i: (i, 0)),
            pl.BlockSpec((_HALO, d), lambda i: (jnp.maximum(i * per_halo - 1, 0), 0)),
            pl.BlockSpec((tm, d), lambda i: (i, 0)),
            pl.BlockSpec((CONV_WIDTH, d), row),
            pl.BlockSpec((1, d), row),
            pl.BlockSpec((1, d), row),
            pl.BlockSpec((1, d), row),
            pl.BlockSpec((d, d), row),
            pl.BlockSpec((1, d), row),
        ],
        out_specs=pl.BlockSpec((tm, d), lambda i: (i, 0)),
        out_shape=jax.ShapeDtypeStruct((n, d), _F32),
        scratch_shapes=[pltpu.VMEM((d // LANES, tm + _HALO, LANES), _F32),
                        pltpu.VMEM((tm, d), _F32),
                        pltpu.VMEM((tm, d), _BF16)],
        compiler_params=_params(1),
        name="conv_out",
    )(u, u, x, dw, dwb, lng, lnb, w2, b2)


def _ffn_kernel(x_ref, g_ref, wg_ref, wu_ref, wd_ref, *rest, final_norm):
    if final_norm:
        gf_ref, o_ref, xn_ref, acc_ref = rest
    else:
        o_ref, xn_ref, acc_ref = rest
    f = pl.program_id(1)

    @pl.when(f == 0)
    def _():
        _rmsnorm_rows(x_ref, [xn_ref], [g_ref])
        acc_ref[...] = x_ref[...]

    xn = xn_ref[...]
    gate = jnp.dot(xn, wg_ref[...], preferred_element_type=_F32)
    up = jnp.dot(xn, wu_ref[...], preferred_element_type=_F32)
    act = (gate * _sigmoid(gate) * up).astype(_BF16)
    acc_ref[...] += jnp.dot(act, wd_ref[...], preferred_element_type=_F32)

    @pl.when(f == pl.num_programs(1) - 1)
    def _():
        if final_norm:
            _rmsnorm_rows(acc_ref, [o_ref], [gf_ref])
        else:
            o_ref[...] = acc_ref[...]


def _ffn(x, g, wg, wu, wd, gf=None, *, tm=512, tf=512):
    n, d = x.shape
    f = wg.shape[1]
    final_norm = gf is not None
    in_specs = [
        pl.BlockSpec((tm, d), lambda i, j: (i, 0)),
        pl.BlockSpec((1, d), lambda i, j: (0, 0)),
        pl.BlockSpec((d, tf), lambda i, j: (0, j)),
        pl.BlockSpec((d, tf), lambda i, j: (0, j)),
        pl.BlockSpec((tf, d), lambda i, j: (j, 0)),
    ]
    args = [x, g, wg, wu, wd]
    if final_norm:
        in_specs.append(pl.BlockSpec((1, d), lambda i, j: (0, 0)))
        args.append(gf)
    return pl.pallas_call(
        functools.partial(_ffn_kernel, final_norm=final_norm),
        grid=(n // tm, f // tf),
        in_specs=in_specs,
        out_specs=pl.BlockSpec((tm, d), lambda i, j: (i, 0)),
        out_shape=jax.ShapeDtypeStruct((n, d), _F32),
        scratch_shapes=[pltpu.VMEM((tm, d), _BF16), pltpu.VMEM((tm, d), _F32)],
        compiler_params=_params(2),
        name="ffn_final" if final_norm else "ffn",
    )(*args)


def _kvq_kernel(x_ref, gkv_ref, gq_ref, w_ref, o_ref, xkv_ref, xq_ref, *, q_scale):
    j = pl.program_id(1)

    @pl.when(j == 0)
    def _():
        _rmsnorm_rows(x_ref, [xkv_ref, xq_ref], [gkv_ref, gq_ref])

    @pl.when(j < 2)
    def _():
        o_ref[...] = jnp.dot(xkv_ref[...], w_ref[...], preferred_element_type=_F32)

    @pl.when(j == 2)
    def _():
        o_ref[...] = jnp.dot(xq_ref[...], w_ref[...], preferred_element_type=_F32) * q_scale


def _kvq(x, gkv, gq, w_kvq, *, q_scale, tm=512):
    n, d = x.shape
    return pl.pallas_call(
        functools.partial(_kvq_kernel, q_scale=q_scale),
        grid=(n // tm, 3),
        in_specs=[
            pl.BlockSpec((tm, d), lambda i, j: (i, 0)),
            pl.BlockSpec((1, d), lambda i, j: (0, 0)),
            pl.BlockSpec((1, d), lambda i, j: (0, 0)),
            pl.BlockSpec((None, d, d), lambda i, j: (j, 0, 0)),
        ],
        out_specs=pl.BlockSpec((None, tm, d), lambda i, j: (j, i, 0)),
        out_shape=jax.ShapeDtypeStruct((3, n, d), _F32),
        scratch_shapes=[pltpu.VMEM((tm, d), _BF16), pltpu.VMEM((tm, d), _BF16)],
        compiler_params=_params(2),
        name="kvq",
    )(x, gkv, gq, w_kvq)


def _step_tables():
    p = np.arange(Q_BLOCK)[:, None]
    c = np.arange(2 * Q_BLOCK)[None, :]
    pair = []
    for shift in (0, Q_BLOCK):
        j = shift + p - c
        valid = (j >= 0) & (j <= WINDOW_STEPS)
        pair.append(np.where(valid, j.astype(np.float32), np.float32(MASKED)))
    return np.stack(pair).astype(np.float32)


def _attention_kernel(slopes_ref, tab_ref, q_ref, k_ref, v_ref, o_ref, bias_ref, ob_ref, lse_ref):
    seq = q_ref.shape[0]
    h = pl.program_id(1)
    neg_slope2 = -slopes_ref[h] * LOG2E
    n_br = len(BRANCH_DILATIONS)
    for b, dil in enumerate(BRANCH_DILATIONS):
        for t in range(2):
            bias_ref[b, t] = tab_ref[t] * (dil * neg_slope2)

    def rows(start, size, dil):
        if dil == 1:
            return pl.ds(pl.multiple_of(start, Q_BLOCK), size)
        return pl.ds(start, size, stride=dil)

    def block(b, dil, res, nb):
        span = dil * Q_BLOCK
        q_rows = rows(res + span * nb, Q_BLOCK, dil)
        k_rows = rows(res + span * jnp.maximum(nb - 1, 0), 2 * Q_BLOCK, dil)
        q = q_ref[q_rows, :].astype(_BF16)
        k = k_ref[k_rows, :].astype(_BF16)
        v = v_ref[k_rows, :].astype(_BF16)
        s = lax.dot_general(q, k, (((1,), (1,)), ((), ())), preferred_element_type=_F32)
        s = s + bias_ref[b, jnp.minimum(nb, 1)]
        m = jnp.max(s, axis=-1, keepdims=True)
        e = jnp.exp2(s - m)
        l = jnp.sum(e, axis=-1, keepdims=True)
        acc = jnp.dot(e.astype(_BF16), v, preferred_element_type=_F32)
        ob_ref[b, q_rows, :] = acc * (1.0 / l)
        lse_ref[b, q_rows, :] = jnp.broadcast_to(m + jnp.log2(l), (Q_BLOCK, LANES))

    g = BLOCKS_PER_STEP
    for b, dil in enumerate(BRANCH_DILATIONS):
        n_blocks = seq // (dil * Q_BLOCK)
        if dil >= g:
            groups = dil // g

            def body(i, carry, b=b, dil=dil, groups=groups):
                nb = i // groups
                res0 = (i % groups) * g
                for u in range(g):
                    block(b, dil, res0 + u, nb)
                return carry

            lax.fori_loop(0, n_blocks * groups, body, 0)
        else:
            assert dil == 1 and n_blocks % g == 0

            def body(i, carry, b=b, dil=dil):
                for u in range(g):
                    block(b, dil, 0, i * g + u)
                return carry

            lax.fori_loop(0, n_blocks // g, body, 0)

    rows_per = 256

    def merge(i, carry):
        r = pl.ds(pl.multiple_of(i * rows_per, rows_per), rows_per)
        ls = [lse_ref[b, r, :] for b in range(n_br)]
        l_max = functools.reduce(jnp.maximum, ls)
        num = 0.0
        den = 0.0
        for b in range(n_br):
            w = jnp.exp2(ls[b] - l_max)
            num = num + w * ob_ref[b, r, :]
            den = den + w
        o_ref[r, :] = num / den
        return carry

    lax.fori_loop(0, seq // rows_per, merge, 0)


def _attention(kvq, slopes, tables, *, batch, seq):
    _, n, d = kvq.shape
    e = d // N_HEADS
    n_br = len(BRANCH_DILATIONS)
    kvq4 = kvq.reshape(3, batch, seq, d)

    def qkv_spec(which):
        return pl.BlockSpec((None, None, seq, e), lambda b, h: (which, b, 0, h))

    return pl.pallas_call(
        _attention_kernel,
        grid=(batch, N_HEADS),
        in_specs=[
            pl.BlockSpec(memory_space=pltpu.SMEM),
            pl.BlockSpec(tables.shape, lambda b, h: (0, 0, 0)),
            qkv_spec(2), qkv_spec(0), qkv_spec(1),
        ],
        out_specs=pl.BlockSpec((None, seq, e), lambda b, h: (b, 0, h)),
        out_shape=jax.ShapeDtypeStruct((batch, seq, d), _F32),
        scratch_shapes=[
            pltpu.VMEM((n_br, 2, Q_BLOCK, 2 * Q_BLOCK), _F32),
            pltpu.VMEM((n_br, seq, e), _F32),
            pltpu.VMEM((n_br, seq, LANES), _F32),
        ],
        compiler_params=_params(2),
        name="dilated_attention",
    )(slopes, tables, kvq4, kvq4, kvq4)


def _out_proj_kernel(a_ref, x_ref, w_ref, o_ref):
    o_ref[...] = x_ref[...] + jnp.dot(a_ref[...].astype(_BF16), w_ref[...],
                                      preferred_element_type=_F32)


def _out_proj(att, x, w, *, tm=512):
    n, d = x.shape
    return pl.pallas_call(
        _out_proj_kernel,
        grid=(n // tm,),
        in_specs=[
            pl.BlockSpec((tm, d), lambda i: (i, 0)),
            pl.BlockSpec((tm, d), lambda i: (i, 0)),
            pl.BlockSpec((d, d), lambda i: (0, 0)),
        ],
        out_specs=pl.BlockSpec((tm, d), lambda i: (i, 0)),
        out_shape=jax.ShapeDtypeStruct((n, d), _F32),
        compiler_params=_params(1),
        name="out_proj",
    )(att, x, w)


def kernel(x, a_norm_g, conv_w1, conv_b1, conv_dw, conv_dw_b, conv_ln_g, conv_ln_b, conv_w2, conv_b2,
           kv_norm_g, w_k, w_v, b_norm_g, w_q, w_o, ffn_norm_g, ffn_w_gate, ffn_w_up, ffn_w_down,
           final_norm_g):
    batch, seq, d = x.shape
    n_a = a_norm_g.shape[0]
    n_b = b_norm_g.shape[0]
    depth = n_a + n_b
    assert d % N_HEADS == 0 and d // N_HEADS == LANES
    assert seq % (max(BRANCH_DILATIONS) * Q_BLOCK * 2) == 0

    row = lambda v: v.reshape(1, -1)
    h = x.reshape(batch * seq, d)
    head_dim = d // N_HEADS
    slopes = jnp.exp2(-8.0 * jnp.arange(1, N_HEADS + 1, dtype=_F32) / N_HEADS)
    tables = jnp.asarray(_step_tables())
    kvq = None

    for layer in range(depth):
        if layer < n_a:
            a = layer
            u = _glu_in(h, row(a_norm_g[a]), conv_w1[a].astype(_BF16), row(conv_b1[a]))
            h = _conv_out(u, h, conv_dw[a], row(conv_dw_b[a]), row(conv_ln_g[a]), row(conv_ln_b[a]),
                          conv_w2[a].astype(_BF16), row(conv_b2[a]), seq=seq)
        else:
            i = layer - n_a
            assert n_b == 1, "one attention layer supported"
            w_kvq = jnp.stack([w_k, w_v, w_q[i]]).astype(_BF16)
            kvq = _kvq(h, row(kv_norm_g), row(b_norm_g[i]), w_kvq,
                       q_scale=head_dim ** -0.5 * LOG2E)
            att = _attention(kvq, slopes, tables, batch=batch, seq=seq)
            h = _out_proj(att.reshape(batch * seq, d), h, w_o[i].astype(_BF16))
        last = layer == depth - 1
        h = _ffn(h, row(ffn_norm_g[layer]), ffn_w_gate[layer].astype(_BF16),
                 ffn_w_up[layer].astype(_BF16), ffn_w_down[layer].astype(_BF16),
                 row(final_norm_g) if last else None)
    return h.reshape(batch, seq, d)
```

```python
import functools
import math

import numpy as np
import jax
import jax.numpy as jnp
from jax import lax
from jax.experimental import pallas as pl
from jax.experimental.pallas import tpu as pltpu

RMS_EPS = 1e-6
LN_EPS = 1e-5
N_HEADS = 16
CONV_WIDTH = 31
BRANCH_DILATIONS = (1, 4, 16)
WINDOW_STEPS = 128
Q_BLOCK = 128
GROUP = 4
MASKED = 1e30
LOG2E = math.log2(math.e)
LANES = 128

_VMEM_LIMIT = 56 * 1024 * 1024
_BF16 = jnp.bfloat16
_F32 = jnp.float32


def _params(n_axes):
    return pltpu.CompilerParams(dimension_semantics=("arbitrary",) * n_axes,
                                vmem_limit_bytes=_VMEM_LIMIT)


def _sigmoid(x):
    return 1.0 / (1.0 + jnp.exp(-x))


def _rmsnorm_rows(x_ref, out_refs, g_refs, chunk=128):
    n = x_ref.shape[0] // chunk

    def body(i, carry):
        r = pl.multiple_of(i * chunk, chunk)
        x = x_ref[pl.ds(r, chunk), :]
        xhat = x * lax.rsqrt(jnp.mean(x * x, axis=-1, keepdims=True) + RMS_EPS)
        for o_ref, g_ref in zip(out_refs, g_refs):
            o_ref[pl.ds(r, chunk), :] = (xhat * g_ref[...]).astype(o_ref.dtype)
        return carry

    lax.fori_loop(0, n, body, 0)


def _glu_in_kernel(x_ref, g_ref, wa_ref, wg_ref, ba_ref, bg_ref, u_ref, xn_ref):
    @pl.when(pl.program_id(1) == 0)
    def _():
        _rmsnorm_rows(x_ref, [xn_ref], [g_ref])

    xn = xn_ref[...]
    a = jnp.dot(xn, wa_ref[...], preferred_element_type=_F32) + ba_ref[...]
    gate = jnp.dot(xn, wg_ref[...], preferred_element_type=_F32) + bg_ref[...]
    u_ref[...] = a * _sigmoid(gate)


def _glu_in(x, g, w1, b1, *, tm=1024, tn=512):
    n, d = x.shape
    nj = d // tn
    return pl.pallas_call(
        _glu_in_kernel,
        grid=(n // tm, nj),
        in_specs=[
            pl.BlockSpec((tm, d), lambda i, j: (i, 0)),
            pl.BlockSpec((1, d), lambda i, j: (0, 0)),
            pl.BlockSpec((d, tn), lambda i, j: (0, j)),
            pl.BlockSpec((d, tn), lambda i, j: (0, j + nj)),
            pl.BlockSpec((1, tn), lambda i, j: (0, j)),
            pl.BlockSpec((1, tn), lambda i, j: (0, j + nj)),
        ],
        out_specs=pl.BlockSpec((tm, tn), lambda i, j: (i, j)),
        out_shape=jax.ShapeDtypeStruct((n, d), _F32),
        scratch_shapes=[pltpu.VMEM((tm, d), _BF16)],
        compiler_params=_params(2),
        name="glu_in",
    )(x, g, w1, w1, b1, b1)


_HALO = 32


def _conv_out_kernel(u_ref, halo_ref, x_ref, dw_ref, dwb_ref, lng_ref, lnb_ref, w2_ref, b2_ref,
                     o_ref, buf_ref, y_ref, z_ref, *, blocks_per_seq):
    tm, d = u_ref.shape
    first = (pl.program_id(0) % blocks_per_seq) == 0
    for c in range(d // LANES):
        lanes = slice(c * LANES, (c + 1) * LANES)
        buf_ref[c, 0:_HALO, :] = jnp.where(first, 0.0, halo_ref[:, lanes])
        buf_ref[c, _HALO:, :] = u_ref[:, lanes]

    def conv_lanes(c, carry):
        lanes = pl.ds(pl.multiple_of(c * LANES, LANES), LANES)
        w = dw_ref[:, lanes]
        acc = jnp.broadcast_to(dwb_ref[:, lanes], (tm, LANES))
        for k in range(CONV_WIDTH):
            shifted = buf_ref[c, pl.ds(_HALO - (CONV_WIDTH - 1) + k, tm, stride=1), :]
            acc = acc + shifted * w[k:k + 1, :]
        y_ref[:, lanes] = acc
        return carry

    lax.fori_loop(0, d // LANES, conv_lanes, 0)

    chunk = 128

    def norm_rows(i, carry):
        rows = pl.ds(pl.multiple_of(i * chunk, chunk), chunk)
        y = y_ref[rows, :]
        mu = jnp.mean(y, axis=-1, keepdims=True)
        yc = y - mu
        var = jnp.mean(yc * yc, axis=-1, keepdims=True)
        zn = yc * lax.rsqrt(var + LN_EPS) * lng_ref[...] + lnb_ref[...]
        z_ref[rows, :] = (zn * _sigmoid(zn)).astype(z_ref.dtype)
        return carry

    lax.fori_loop(0, tm // chunk, norm_rows, 0)

    o_ref[...] = (x_ref[...] + b2_ref[...]
                  + jnp.dot(z_ref[...], w2_ref[...], preferred_element_type=_F32))


def _conv_out(u, x, dw, dwb, lng, lnb, w2, b2, *, seq, tm=256):
    n, d = u.shape
    per_halo = tm // _HALO
    kern = functools.partial(_conv_out_kernel, blocks_per_seq=seq // tm)
    row = lambda i: (0, 0)
    return pl.pallas_call(
        kern,
        grid=(n // tm,),
        in_specs=[
            pl.BlockSpec((tm, d), lambda i: (i, 0)),
            pl.BlockSpec((_HALO, d), lambda i: (jnp.maximum(i * per_halo - 1, 0), 0)),
            pl.BlockSpec((tm, d), lambda i: (i, 0)),
            pl.BlockSpec((CONV_WIDTH, d), row),
            pl.BlockSpec((1, d), row),
            pl.BlockSpec((1, d), row),
            pl.BlockSpec((1, d), row),
            pl.BlockSpec((d, d), row),
            pl.BlockSpec((1, d), row),
        ],
        out_specs=pl.BlockSpec((tm, d), lambda i: (i, 0)),
        out_shape=jax.ShapeDtypeStruct((n, d), _F32),
        scratch_shapes=[pltpu.VMEM((d // LANES, tm + _HALO, LANES), _F32),
                        pltpu.VMEM((tm, d), _F32),
                        pltpu.VMEM((tm, d), _BF16)],
        compiler_params=_params(1),
        name="conv_out",
    )(u, u, x, dw, dwb, lng, lnb, w2, b2)


def _ffn_kernel(x_ref, g_ref, wg_ref, wu_ref, wd_ref, *rest, final_norm):
    if final_norm:
        gf_ref, o_ref, xn_ref = rest
    else:
        o_ref, xn_ref = rest
    f = pl.program_id(1)

    @pl.when(f == 0)
    def _():
        _rmsnorm_rows(x_ref, [xn_ref], [g_ref])
        o_ref[...] = x_ref[...]

    xn = xn_ref[...]
    gate = jnp.dot(xn, wg_ref[...], preferred_element_type=_F32)
    up = jnp.dot(xn, wu_ref[...], preferred_element_type=_F32)
    act = (gate * _sigmoid(gate) * up).astype(_BF16)
    o_ref[...] += jnp.dot(act, wd_ref[...], preferred_element_type=_F32)

    if final_norm:
        @pl.when(f == pl.num_programs(1) - 1)
        def _():
            _rmsnorm_rows(o_ref, [o_ref], [gf_ref])


def _ffn(x, g, wg, wu, wd, gf=None, *, tm=1024, tf=512):
    n, d = x.shape
    f = wg.shape[1]
    final_norm = gf is not None
    in_specs = [
        pl.BlockSpec((tm, d), lambda i, j: (i, 0)),
        pl.BlockSpec((1, d), lambda i, j: (0, 0)),
        pl.BlockSpec((d, tf), lambda i, j: (0, j)),
        pl.BlockSpec((d, tf), lambda i, j: (0, j)),
        pl.BlockSpec((tf, d), lambda i, j: (j, 0)),
    ]
    args = [x, g, wg, wu, wd]
    if final_norm:
        in_specs.append(pl.BlockSpec((1, d), lambda i, j: (0, 0)))
        args.append(gf)
    return pl.pallas_call(
        functools.partial(_ffn_kernel, final_norm=final_norm),
        grid=(n // tm, f // tf),
        in_specs=in_specs,
        out_specs=pl.BlockSpec((tm, d), lambda i, j: (i, 0)),
        out_shape=jax.ShapeDtypeStruct((n, d), _F32),
        scratch_shapes=[pltpu.VMEM((tm, d), _BF16)],
        compiler_params=_params(2),
        name="ffn_final" if final_norm else "ffn",
    )(*args)


def _kvq_kernel(x_ref, gkv_ref, gq_ref, w_ref, o_ref, xkv_ref, xq_ref, *, q_scale, blocks_per_proj):
    j = pl.program_id(1)

    @pl.when(j == 0)
    def _():
        _rmsnorm_rows(x_ref, [xkv_ref, xq_ref], [gkv_ref, gq_ref], chunk=64)

    @pl.when(j < 2 * blocks_per_proj)
    def _():
        o_ref[...] = jnp.dot(xkv_ref[...], w_ref[...], preferred_element_type=_F32)

    @pl.when(j >= 2 * blocks_per_proj)
    def _():
        o_ref[...] = jnp.dot(xq_ref[...], w_ref[...], preferred_element_type=_F32) * q_scale


def _kvq(x, gkv, gq, w_kvq, *, q_scale, tm=1024, tn=1024):
    n, d = x.shape
    per = d // tn
    return pl.pallas_call(
        functools.partial(_kvq_kernel, q_scale=q_scale, blocks_per_proj=per),
        grid=(n // tm, 3 * per),
        in_specs=[
            pl.BlockSpec((tm, d), lambda i, j: (i, 0)),
            pl.BlockSpec((1, d), lambda i, j: (0, 0)),
            pl.BlockSpec((1, d), lambda i, j: (0, 0)),
            pl.BlockSpec((None, d, tn), lambda i, j: (j // per, 0, j % per)),
        ],
        out_specs=pl.BlockSpec((None, tm, tn), lambda i, j: (j // per, i, j % per)),
        out_shape=jax.ShapeDtypeStruct((3, n, d), _F32),
        scratch_shapes=[pltpu.VMEM((tm, d), _BF16), pltpu.VMEM((tm, d), _BF16)],
        compiler_params=_params(2),
        name="kvq",
    )(x, gkv, gq, w_kvq)


def _step_tables():
    p = np.arange(Q_BLOCK)[:, None]
    c = np.arange(2 * Q_BLOCK)[None, :]
    pair = []
    for shift in (0, Q_BLOCK):
        j = shift + p - c
        valid = (j >= 0) & (j <= WINDOW_STEPS)
        pair.append(np.where(valid, j.astype(np.float32), np.float32(MASKED)))
    return np.stack(pair).astype(np.float32)


def _attention_kernel(slopes_ref, tab_ref, q_ref, k_ref, v_ref, o_ref,
                      bias_ref, qs_ref, ks_ref, vs_ref, t4_ref, s_ref, e_ref, rl_ref, ob_ref, lse_ref):
    seq = q_ref.shape[0]
    blocks_per_branch = seq // Q_BLOCK
    n_br = len(BRANCH_DILATIONS)
    n_groups = n_br * blocks_per_branch // GROUP
    assert blocks_per_branch % GROUP == 0 and n_groups % 2 == 0 and n_groups >= 4
    h = pl.program_id(1)
    neg_slope2 = -slopes_ref[h] * LOG2E
    for b, dil in enumerate(BRANCH_DILATIONS):
        for t in range(2):
            bias_ref[b, t] = tab_ref[t] * (dil * neg_slope2)

    chunk = seq // 16
    quarter = seq // 4

    def finer(c):
        return pl.ds((c // 4) + 4 * (c % 4) * chunk, chunk, stride=4)

    def finer16(c):
        return pl.ds((c % 4) * quarter + c // 4, chunk, stride=4)

    srcs = (q_ref, k_ref, v_ref)
    dsts = (qs_ref, ks_ref, vs_ref)

    def regroup4(c, carry):
        dst = pl.ds(pl.multiple_of(c * chunk, chunk), chunk)
        for a in range(3):
            dsts[a][0, dst, :] = srcs[a][dst, :].astype(_BF16)
            by4 = srcs[a][finer(c), :]
            t4_ref[a, dst, :] = by4
            dsts[a][1, dst, :] = by4.astype(_BF16)
        return carry

    lax.fori_loop(0, seq // chunk, regroup4, 0)

    def regroup16(c, carry):
        dst = pl.ds(pl.multiple_of(c * chunk, chunk), chunk)
        for a in range(3):
            dsts[a][2, dst, :] = t4_ref[a, finer16(c), :].astype(_BF16)
        return carry

    lax.fori_loop(0, seq // chunk, regroup16, 0)

    def locate(g, u):
        t = g * GROUP + u
        b = t // blocks_per_branch
        jb = t % blocks_per_branch
        per_class = lax.shift_right_logical(blocks_per_branch, 2 * b)
        is_first = (jb & (per_class - 1)) == 0
        q0 = pl.multiple_of(jb * Q_BLOCK, Q_BLOCK)
        k0 = pl.multiple_of(jnp.where(is_first, q0, q0 - Q_BLOCK), Q_BLOCK)
        return b, q0, k0, jnp.where(is_first, 0, 1)

    def scores(g, slot):
        for u in range(GROUP):
            b, q0, k0, _ = locate(g, u)
            q = qs_ref[b, pl.ds(q0, Q_BLOCK), :]
            k = ks_ref[b, pl.ds(k0, 2 * Q_BLOCK), :]
            s_ref[slot, u] = lax.dot_general(q, k, (((1,), (1,)), ((), ())),
                                             preferred_element_type=_F32)

    def softmax(g, slot):
        for u in range(GROUP):
            b, q0, _, tab = locate(g, u)
            s = s_ref[slot, u] + bias_ref[b, tab]
            m = jnp.max(s, axis=-1, keepdims=True)
            e = jnp.exp2(s - m)
            l = jnp.sum(e, axis=-1, keepdims=True)
            e_ref[slot, u] = e.astype(_BF16)
            rl_ref[slot, u] = jnp.broadcast_to(1.0 / l, (Q_BLOCK, LANES))
            lse_ref[b, pl.ds(q0, Q_BLOCK), :] = jnp.broadcast_to(m + jnp.log2(l), (Q_BLOCK, LANES))

    def values(g, slot):
        for u in range(GROUP):
            b, q0, k0, _ = locate(g, u)
            v = vs_ref[b, pl.ds(k0, 2 * Q_BLOCK), :]
            acc = jnp.dot(e_ref[slot, u], v, preferred_element_type=_F32)
            ob_ref[b, pl.ds(q0, Q_BLOCK), :] = acc * rl_ref[slot, u]

    zero = jnp.int32(0)
    scores(zero, 0)
    softmax(zero, 0)
    scores(zero + 1, 1)

    def trip(i, carry):
        g = 2 * i
        values(g - 2, 0)
        softmax(g - 1, 1)
        scores(g, 0)
        values(g - 1, 1)
        softmax(g, 0)
        scores(g + 1, 1)
        return carry

    lax.fori_loop(1, n_groups // 2, trip, 0)
    values(zero + (n_groups - 2), 0)
    softmax(zero + (n_groups - 1), 1)
    values(zero + (n_groups - 1), 1)

    def coarsen(c, carry):
        src = pl.ds(pl.multiple_of(c * chunk, chunk), chunk)
        t4_ref[0, finer16(c), :] = ob_ref[2, src, :]
        t4_ref[1, finer16(c), :] = lse_ref[2, src, :]
        return carry

    lax.fori_loop(0, seq // chunk, coarsen, 0)

    rows_per = 128

    def merge(c, carry):
        by4 = pl.ds(pl.multiple_of(c * rows_per, rows_per), rows_per)
        per_class = quarter // rows_per
        natural = pl.ds((c // per_class) + 4 * (c % per_class) * rows_per, rows_per, stride=4)
        ls = [lse_ref[0, natural, :], lse_ref[1, by4, :], t4_ref[1, by4, :]]
        os = [ob_ref[0, natural, :], ob_ref[1, by4, :], t4_ref[0, by4, :]]
        l_max = functools.reduce(jnp.maximum, ls)
        num = 0.0
        den = 0.0
        for b in range(n_br):
            w = jnp.exp2(ls[b] - l_max)
            num = num + w * os[b]
            den = den + w
        o_ref[natural, :] = num / den
        return carry

    lax.fori_loop(0, seq // rows_per, merge, 0)


def _attention(kvq, slopes, tables, *, batch, seq):
    _, n, d = kvq.shape
    e = d // N_HEADS
    n_br = len(BRANCH_DILATIONS)
    kvq4 = kvq.reshape(3, batch, seq, d)

    def qkv_spec(which):
        return pl.BlockSpec((None, None, seq, e), lambda b, h: (which, b, 0, h))

    return pl.pallas_call(
        _attention_kernel,
        grid=(batch, N_HEADS),
        in_specs=[
            pl.BlockSpec(memory_space=pltpu.SMEM),
            pl.BlockSpec(tables.shape, lambda b, h: (0, 0, 0)),
            qkv_spec(2), qkv_spec(0), qkv_spec(1),
        ],
        out_specs=pl.BlockSpec((None, seq, e), lambda b, h: (b, 0, h)),
        out_shape=jax.ShapeDtypeStruct((batch, seq, d), _F32),
        scratch_shapes=[
            pltpu.VMEM((n_br, 2, Q_BLOCK, 2 * Q_BLOCK), _F32),
            pltpu.VMEM((n_br, seq, e), _BF16),
            pltpu.VMEM((n_br, seq, e), _BF16),
            pltpu.VMEM((n_br, seq, e), _BF16),
            pltpu.VMEM((3, seq, e), _F32),
            pltpu.VMEM((2, GROUP, Q_BLOCK, 2 * Q_BLOCK), _F32),
            pltpu.VMEM((2, GROUP, Q_BLOCK, 2 * Q_BLOCK), _BF16),
            pltpu.VMEM((2, GROUP, Q_BLOCK, LANES), _F32),
            pltpu.VMEM((n_br, seq, e), _F32),
            pltpu.VMEM((n_br, seq, LANES), _F32),
        ],
        compiler_params=_params(2),
        name="dilated_attention",
    )(slopes, tables, kvq4, kvq4, kvq4)


def _out_proj_kernel(a_ref, x_ref, w_ref, o_ref):
    o_ref[...] = x_ref[...] + jnp.dot(a_ref[...].astype(_BF16), w_ref[...],
                                      preferred_element_type=_F32)


def _out_proj(att, x, w, *, tm=512):
    n, d = x.shape
    return pl.pallas_call(
        _out_proj_kernel,
        grid=(n // tm,),
        in_specs=[
            pl.BlockSpec((tm, d), lambda i: (i, 0)),
            pl.BlockSpec((tm, d), lambda i: (i, 0)),
            pl.BlockSpec((d, d), lambda i: (0, 0)),
        ],
        out_specs=pl.BlockSpec((tm, d), lambda i: (i, 0)),
        out_shape=jax.ShapeDtypeStruct((n, d), _F32),
        compiler_params=_params(1),
        name="out_proj",
    )(att, x, w)


def kernel(x, a_norm_g, conv_w1, conv_b1, conv_dw, conv_dw_b, conv_ln_g, conv_ln_b, conv_w2, conv_b2,
           kv_norm_g, w_k, w_v, b_norm_g, w_q, w_o, ffn_norm_g, ffn_w_gate, ffn_w_up, ffn_w_down,
           final_norm_g):
    batch, seq, d = x.shape
    n_a = a_norm_g.shape[0]
    n_b = b_norm_g.shape[0]
    depth = n_a + n_b
    assert d % N_HEADS == 0 and d // N_HEADS == LANES
    assert BRANCH_DILATIONS == (1, 4, 16) and seq % (max(BRANCH_DILATIONS) * Q_BLOCK * 2) == 0

    row = lambda v: v.reshape(1, -1)
    h = x.reshape(batch * seq, d)
    head_dim = d // N_HEADS
    slopes = jnp.exp2(-8.0 * jnp.arange(1, N_HEADS + 1, dtype=_F32) / N_HEADS)
    tables = jnp.asarray(_step_tables())
    kvq = None

    for layer in range(depth):
        if layer < n_a:
            a = layer
            u = _glu_in(h, row(a_norm_g[a]), conv_w1[a].astype(_BF16), row(conv_b1[a]))
            h = _conv_out(u, h, conv_dw[a], row(conv_dw_b[a]), row(conv_ln_g[a]), row(conv_ln_b[a]),
                          conv_w2[a].astype(_BF16), row(conv_b2[a]), seq=seq)
        else:
            i = layer - n_a
            assert n_b == 1, "one attention layer supported"
            w_kvq = jnp.stack([w_k, w_v, w_q[i]]).astype(_BF16)
            kvq = _kvq(h, row(kv_norm_g), row(b_norm_g[i]), w_kvq,
                       q_scale=head_dim ** -0.5 * LOG2E)
            att = _attention(kvq, slopes, tables, batch=batch, seq=seq)
            h = _out_proj(att.reshape(batch * seq, d), h, w_o[i].astype(_BF16))
        last = layer == depth - 1
        h = _ffn(h, row(ffn_norm_g[layer]), ffn_w_gate[layer].astype(_BF16),
                 ffn_w_up[layer].astype(_BF16), ffn_w_down[layer].astype(_BF16),
                 row(final_norm_g) if last else None)
    return h.reshape(batch, seq, d)
```

```python
import functools
import math

import numpy as np
import jax
import jax.numpy as jnp
from jax import lax
from jax.experimental import pallas as pl
from jax.experimental.pallas import tpu as pltpu

RMS_EPS = 1e-6
LN_EPS = 1e-5
N_HEADS = 16
CONV_WIDTH = 31
BRANCH_DILATIONS = (1, 4, 16)
WINDOW_STEPS = 128
Q_BLOCK = 128
GROUP = 4
MASKED = 1e30
LOG2E = math.log2(math.e)
LANES = 128

_VMEM_LIMIT = 56 * 1024 * 1024
_BF16 = jnp.bfloat16
_F32 = jnp.float32


def _params(n_axes):
    return pltpu.CompilerParams(dimension_semantics=("arbitrary",) * n_axes,
                                vmem_limit_bytes=_VMEM_LIMIT)


def _sigmoid(x):
    return 1.0 / (1.0 + jnp.exp(-x))


def _rmsnorm_rows(x_ref, out_refs, g_refs, chunk=128):
    n = x_ref.shape[0] // chunk

    def body(i, carry):
        r = pl.multiple_of(i * chunk, chunk)
        x = x_ref[pl.ds(r, chunk), :]
        xhat = x * lax.rsqrt(jnp.mean(x * x, axis=-1, keepdims=True) + RMS_EPS)
        for o_ref, g_ref in zip(out_refs, g_refs):
            o_ref[pl.ds(r, chunk), :] = (xhat * g_ref[...]).astype(o_ref.dtype)
        return carry

    lax.fori_loop(0, n, body, 0)


def _cast_kernel(*refs):
    *src_refs, o_ref = refs
    which = pl.program_id(0)
    for k, src_ref in enumerate(src_refs):
        @pl.when(which == k)
        def _(src_ref=src_ref):
            o_ref[...] = src_ref[...].astype(o_ref.dtype)


def _cast_stack(sources, *, block_elems=1 << 20):
    rows, cols = sources[0][0].shape[-2:]
    rb = 1 << ((block_elems // cols).bit_length() - 1)
    while rows % rb:
        rb //= 2
    assert rb >= 16
    nb = rows // rb

    def in_spec(k, layer):
        def block(which, b):
            return jnp.where(which == k, b, jnp.where(which < k, 0, nb - 1))
        if layer is None:
            return pl.BlockSpec((rb, cols), lambda which, b: (block(which, b), 0))
        return pl.BlockSpec((None, rb, cols), lambda which, b: (layer, block(which, b), 0))

    return pl.pallas_call(
        _cast_kernel,
        grid=(len(sources), nb),
        in_specs=[in_spec(k, layer) for k, (_, layer) in enumerate(sources)],
        out_specs=pl.BlockSpec((None, rb, cols), lambda which, b: (which, b, 0)),
        out_shape=jax.ShapeDtypeStruct((len(sources), rows, cols), _BF16),
        compiler_params=_params(2),
        name="cast_bf16",
    )(*[a for a, _ in sources])


def _glu_in_kernel(x_ref, g_ref, wa_ref, wg_ref, ba_ref, bg_ref, u_ref, xn_ref):
    @pl.when(pl.program_id(1) == 0)
    def _():
        _rmsnorm_rows(x_ref, [xn_ref], [g_ref])

    xn = xn_ref[...]
    a = jnp.dot(xn, wa_ref[...], preferred_element_type=_F32) + ba_ref[...]
    gate = jnp.dot(xn, wg_ref[...], preferred_element_type=_F32) + bg_ref[...]
    u_ref[...] = a * _sigmoid(gate)


def _glu_in(x, g, w1, b1, *, tm=1024, tn=1024):
    n, d = x.shape
    nj = d // tn
    return pl.pallas_call(
        _glu_in_kernel,
        grid=(n // tm, nj),
        in_specs=[
            pl.BlockSpec((tm, d), lambda i, j: (i, 0)),
            pl.BlockSpec((1, d), lambda i, j: (0, 0)),
            pl.BlockSpec((d, tn), lambda i, j: (0, j)),
            pl.BlockSpec((d, tn), lambda i, j: (0, j + nj)),
            pl.BlockSpec((1, tn), lambda i, j: (0, j)),
            pl.BlockSpec((1, tn), lambda i, j: (0, j + nj)),
        ],
        out_specs=pl.BlockSpec((tm, tn), lambda i, j: (i, j)),
        out_shape=jax.ShapeDtypeStruct((n, d), _F32),
        scratch_shapes=[pltpu.VMEM((tm, d), _BF16)],
        compiler_params=_params(2),
        name="glu_in",
    )(x, g, w1, w1, b1, b1)


_HALO = 32
_CONV_ROWS = 256


def _conv_out_kernel(u_ref, halo_ref, x_ref, dw_ref, dwb_ref, lng_ref, lnb_ref, w2_ref, b2_ref,
                     o_ref, buf_ref, y_ref, z_ref, *, blocks_per_seq):
    tm, d = u_ref.shape
    first = (pl.program_id(0) % blocks_per_seq) == 0
    for c in range(d // LANES):
        lanes = slice(c * LANES, (c + 1) * LANES)
        buf_ref[c, 0:_HALO, :] = jnp.where(first, 0.0, halo_ref[:, lanes])
        buf_ref[c, _HALO:, :] = u_ref[:, lanes]

    def conv_lanes(c, carry):
        lanes = pl.ds(pl.multiple_of(c * LANES, LANES), LANES)
        w = dw_ref[:, lanes]
        for r0 in range(0, tm, _CONV_ROWS):
            acc = jnp.broadcast_to(dwb_ref[:, lanes], (_CONV_ROWS, LANES))
            for k in range(CONV_WIDTH):
                shifted = buf_ref[c, pl.ds(r0 + _HALO - (CONV_WIDTH - 1) + k, _CONV_ROWS, stride=1), :]
                acc = acc + shifted * w[k:k + 1, :]
            y_ref[r0:r0 + _CONV_ROWS, lanes] = acc
        return carry

    lax.fori_loop(0, d // LANES, conv_lanes, 0)

    chunk = 128

    def norm_rows(i, carry):
        rows = pl.ds(pl.multiple_of(i * chunk, chunk), chunk)
        y = y_ref[rows, :]
        mu = jnp.mean(y, axis=-1, keepdims=True)
        yc = y - mu
        var = jnp.mean(yc * yc, axis=-1, keepdims=True)
        zn = yc * lax.rsqrt(var + LN_EPS) * lng_ref[...] + lnb_ref[...]
        z_ref[rows, :] = (zn * _sigmoid(zn)).astype(z_ref.dtype)
        return carry

    lax.fori_loop(0, tm // chunk, norm_rows, 0)

    o_ref[...] = (x_ref[...] + b2_ref[...]
                  + jnp.dot(z_ref[...], w2_ref[...], preferred_element_type=_F32))


def _conv_out(u, x, dw, dwb, lng, lnb, w2, b2, *, seq, tm=512):
    n, d = u.shape
    per_halo = tm // _HALO
    kern = functools.partial(_conv_out_kernel, blocks_per_seq=seq // tm)
    row = lambda i: (0, 0)
    return pl.pallas_call(
        kern,
        grid=(n // tm,),
        in_specs=[
            pl.BlockSpec((tm, d), lambda i: (i, 0)),
            pl.BlockSpec((_HALO, d), lambda i: (jnp.maximum(i * per_halo - 1, 0), 0)),
            pl.BlockSpec((tm, d), lambda i: (i, 0)),
            pl.BlockSpec((CONV_WIDTH, d), row),
            pl.BlockSpec((1, d), row),
            pl.BlockSpec((1, d), row),
            pl.BlockSpec((1, d), row),
            pl.BlockSpec((d, d), row, pipeline_mode=pl.Buffered(1)),
            pl.BlockSpec((1, d), row),
        ],
        out_specs=pl.BlockSpec((tm, d), lambda i: (i, 0)),
        out_shape=jax.ShapeDtypeStruct((n, d), _F32),
        scratch_shapes=[pltpu.VMEM((d // LANES, tm + _HALO, LANES), _F32),
                        pltpu.VMEM((tm, d), _F32),
                        pltpu.VMEM((tm, d), _BF16)],
        compiler_params=_params(1),
        name="conv_out",
    )(u, u, x, dw, dwb, lng, lnb, w2, b2)


def _ffn_kernel(x_ref, g_ref, wg_ref, wu_ref, wd_ref, *rest, final_norm):
    if final_norm:
        gf_ref, o_ref, xn_ref = rest
    else:
        o_ref, xn_ref = rest
    f = pl.program_id(1)

    @pl.when(f == 0)
    def _():
        _rmsnorm_rows(x_ref, [xn_ref], [g_ref])
        o_ref[...] = x_ref[...]

    xn = xn_ref[...]
    gate = jnp.dot(xn, wg_ref[...], preferred_element_type=_F32)
    up = jnp.dot(xn, wu_ref[...], preferred_element_type=_F32)
    act = (gate * _sigmoid(gate) * up).astype(_BF16)
    o_ref[...] += jnp.dot(act, wd_ref[...], preferred_element_type=_F32)

    if final_norm:
        @pl.when(f == pl.num_programs(1) - 1)
        def _():
            _rmsnorm_rows(o_ref, [o_ref], [gf_ref])


def _ffn(x, g, w_gate_up, w_down, gf=None, *, tm=1024, tf=512):
    n, d = x.shape
    f = w_gate_up.shape[2]
    final_norm = gf is not None
    in_specs = [
        pl.BlockSpec((tm, d), lambda i, j: (i, 0)),
        pl.BlockSpec((1, d), lambda i, j: (0, 0)),
        pl.BlockSpec((None, d, tf), lambda i, j: (0, 0, j)),
        pl.BlockSpec((None, d, tf), lambda i, j: (1, 0, j)),
        pl.BlockSpec((None, tf, d), lambda i, j: (0, j, 0)),
    ]
    args = [x, g, w_gate_up, w_gate_up, w_down]
    if final_norm:
        in_specs.append(pl.BlockSpec((1, d), lambda i, j: (0, 0)))
        args.append(gf)
    return pl.pallas_call(
        functools.partial(_ffn_kernel, final_norm=final_norm),
        grid=(n // tm, f // tf),
        in_specs=in_specs,
        out_specs=pl.BlockSpec((tm, d), lambda i, j: (i, 0)),
        out_shape=jax.ShapeDtypeStruct((n, d), _F32),
        scratch_shapes=[pltpu.VMEM((tm, d), _BF16)],
        compiler_params=_params(2),
        name="ffn_final" if final_norm else "ffn",
    )(*args)


def _kvq_kernel(x_ref, gkv_ref, gq_ref, w_ref, o_ref, xkv_ref, xq_ref, *, q_scale, blocks_per_proj):
    j = pl.program_id(1)

    @pl.when(j == 0)
    def _():
        _rmsnorm_rows(x_ref, [xkv_ref, xq_ref], [gkv_ref, gq_ref], chunk=64)

    @pl.when(j < 2 * blocks_per_proj)
    def _():
        o_ref[...] = jnp.dot(xkv_ref[...], w_ref[...], preferred_element_type=_F32)

    @pl.when(j >= 2 * blocks_per_proj)
    def _():
        o_ref[...] = jnp.dot(xq_ref[...], w_ref[...], preferred_element_type=_F32) * q_scale


def _kvq(x, gkv, gq, w_kvq, *, q_scale, tm=1024, tn=1024):
    n, d = x.shape
    per = d // tn
    return pl.pallas_call(
        functools.partial(_kvq_kernel, q_scale=q_scale, blocks_per_proj=per),
        grid=(n // tm, 3 * per),
        in_specs=[
            pl.BlockSpec((tm, d), lambda i, j: (i, 0)),
            pl.BlockSpec((1, d), lambda i, j: (0, 0)),
            pl.BlockSpec((1, d), lambda i, j: (0, 0)),
            pl.BlockSpec((None, d, tn), lambda i, j: (j // per, 0, j % per)),
        ],
        out_specs=pl.BlockSpec((None, tm, tn), lambda i, j: (j // per, i, j % per)),
        out_shape=jax.ShapeDtypeStruct((3, n, d), _F32),
        scratch_shapes=[pltpu.VMEM((tm, d), _BF16), pltpu.VMEM((tm, d), _BF16)],
        compiler_params=_params(2),
        name="kvq",
    )(x, gkv, gq, w_kvq)


def _step_tables():
    p = np.arange(Q_BLOCK)[:, None]
    c = np.arange(2 * Q_BLOCK)[None, :]
    pair = []
    for shift in (0, Q_BLOCK):
        j = shift + p - c
        valid = (j >= 0) & (j <= WINDOW_STEPS)
        pair.append(np.where(valid, j.astype(np.float32), np.float32(MASKED)))
    return np.stack(pair).astype(np.float32)


def _attention_kernel(slopes_ref, tab_ref, q_ref, k_ref, v_ref, o_ref,
                      bias_ref, qs_ref, ks_ref, vs_ref, t4_ref, s_ref, e_ref, rl_ref, ob_ref, lse_ref):
    seq = q_ref.shape[0]
    blocks_per_branch = seq // Q_BLOCK
    n_br = len(BRANCH_DILATIONS)
    n_groups = n_br * blocks_per_branch // GROUP
    assert blocks_per_branch % GROUP == 0 and n_groups % 2 == 0 and n_groups >= 4
    h = pl.program_id(1)
    neg_slope2 = -slopes_ref[h] * LOG2E
    for b, dil in enumerate(BRANCH_DILATIONS):
        for t in range(2):
            bias_ref[b, t] = tab_ref[t] * (dil * neg_slope2)

    chunk = seq // 16
    quarter = seq // 4

    def finer(c):
        return pl.ds((c // 4) + 4 * (c % 4) * chunk, chunk, stride=4)

    def finer16(c):
        return pl.ds((c % 4) * quarter + c // 4, chunk, stride=4)

    srcs = (q_ref, k_ref, v_ref)
    dsts = (qs_ref, ks_ref, vs_ref)

    def regroup4(c, carry):
        dst = pl.ds(pl.multiple_of(c * chunk, chunk), chunk)
        for a in range(3):
            dsts[a][0, dst, :] = srcs[a][dst, :].astype(_BF16)
            by4 = srcs[a][finer(c), :]
            t4_ref[a, dst, :] = by4
            dsts[a][1, dst, :] = by4.astype(_BF16)
        return carry

    lax.fori_loop(0, seq // chunk, regroup4, 0)

    def regroup16(c, carry):
        dst = pl.ds(pl.multiple_of(c * chunk, chunk), chunk)
        for a in range(3):
            dsts[a][2, dst, :] = t4_ref[a, finer16(c), :].astype(_BF16)
        return carry

    lax.fori_loop(0, seq // chunk, regroup16, 0)

    def locate(g, u):
        t = g * GROUP + u
        b = t // blocks_per_branch
        jb = t % blocks_per_branch
        per_class = lax.shift_right_logical(blocks_per_branch, 2 * b)
        is_first = (jb & (per_class - 1)) == 0
        q0 = pl.multiple_of(jb * Q_BLOCK, Q_BLOCK)
        k0 = pl.multiple_of(jnp.where(is_first, q0, q0 - Q_BLOCK), Q_BLOCK)
        return b, q0, k0, jnp.where(is_first, 0, 1)

    def scores(g, slot):
        for u in range(GROUP):
            b, q0, k0, _ = locate(g, u)
            q = qs_ref[b, pl.ds(q0, Q_BLOCK), :]
            k = ks_ref[b, pl.ds(k0, 2 * Q_BLOCK), :]
            s_ref[slot, u] = lax.dot_general(q, k, (((1,), (1,)), ((), ())),
                                             preferred_element_type=_F32)

    def softmax(g, slot):
        for u in range(GROUP):
            b, q0, _, tab = locate(g, u)
            s = s_ref[slot, u] + bias_ref[b, tab]
            m = jnp.max(s, axis=-1, keepdims=True)
            e = jnp.exp2(s - m)
            l = jnp.sum(e, axis=-1, keepdims=True)
            e_ref[slot, u] = e.astype(_BF16)
            rl_ref[slot, u] = jnp.broadcast_to(1.0 / l, (Q_BLOCK, LANES))
            lse_ref[b, pl.ds(q0, Q_BLOCK), :] = jnp.broadcast_to(m + jnp.log2(l), (Q_BLOCK, LANES))

    def values(g, slot):
        for u in range(GROUP):
            b, q0, k0, _ = locate(g, u)
            v = vs_ref[b, pl.ds(k0, 2 * Q_BLOCK), :]
            acc = jnp.dot(e_ref[slot, u], v, preferred_element_type=_F32)
            ob_ref[b, pl.ds(q0, Q_BLOCK), :] = acc * rl_ref[slot, u]

    zero = jnp.int32(0)
    scores(zero, 0)
    softmax(zero, 0)
    scores(zero + 1, 1)

    def trip(i, carry):
        g = 2 * i
        values(g - 2, 0)
        softmax(g - 1, 1)
        scores(g, 0)
        values(g - 1, 1)
        softmax(g, 0)
        scores(g + 1, 1)
        return carry

    lax.fori_loop(1, n_groups // 2, trip, 0)
    values(zero + (n_groups - 2), 0)
    softmax(zero + (n_groups - 1), 1)
    values(zero + (n_groups - 1), 1)

    def coarsen(c, carry):
        src = pl.ds(pl.multiple_of(c * chunk, chunk), chunk)
        t4_ref[0, finer16(c), :] = ob_ref[2, src, :]
        t4_ref[1, finer16(c), :] = lse_ref[2, src, :]
        return carry

    lax.fori_loop(0, seq // chunk, coarsen, 0)

    rows_per = 128

    def merge(c, carry):
        by4 = pl.ds(pl.multiple_of(c * rows_per, rows_per), rows_per)
        per_class = quarter // rows_per
        natural = pl.ds((c // per_class) + 4 * (c % per_class) * rows_per, rows_per, stride=4)
        ls = [lse_ref[0, natural, :], lse_ref[1, by4, :], t4_ref[1, by4, :]]
        os = [ob_ref[0, natural, :], ob_ref[1, by4, :], t4_ref[0, by4, :]]
        l_max = functools.reduce(jnp.maximum, ls)
        num = 0.0
        den = 0.0
        for b in range(n_br):
            w = jnp.exp2(ls[b] - l_max)
            num = num + w * os[b]
            den = den + w
        o_ref[natural, :] = num / den
        return carry

    lax.fori_loop(0, seq // rows_per, merge, 0)


def _attention(kvq, slopes, tables, *, batch, seq):
    _, n, d = kvq.shape
    e = d // N_HEADS
    n_br = len(BRANCH_DILATIONS)
    kvq4 = kvq.reshape(3, batch, seq, d)

    def qkv_spec(which):
        return pl.BlockSpec((None, None, seq, e), lambda b, h: (which, b, 0, h))

    return pl.pallas_call(
        _attention_kernel,
        grid=(batch, N_HEADS),
        in_specs=[
            pl.BlockSpec(memory_space=pltpu.SMEM),
            pl.BlockSpec(tables.shape, lambda b, h: (0, 0, 0)),
            qkv_spec(2), qkv_spec(0), qkv_spec(1),
        ],
        out_specs=pl.BlockSpec((None, seq, e), lambda b, h: (b, 0, h)),
        out_shape=jax.ShapeDtypeStruct((batch, seq, d), _F32),
        scratch_shapes=[
            pltpu.VMEM((n_br, 2, Q_BLOCK, 2 * Q_BLOCK), _F32),
            pltpu.VMEM((n_br, seq, e), _BF16),
            pltpu.VMEM((n_br, seq, e), _BF16),
            pltpu.VMEM((n_br, seq, e), _BF16),
            pltpu.VMEM((3, seq, e), _F32),
            pltpu.VMEM((2, GROUP, Q_BLOCK, 2 * Q_BLOCK), _F32),
            pltpu.VMEM((2, GROUP, Q_BLOCK, 2 * Q_BLOCK), _BF16),
            pltpu.VMEM((2, GROUP, Q_BLOCK, LANES), _F32),
            pltpu.VMEM((n_br, seq, e), _F32),
            pltpu.VMEM((n_br, seq, LANES), _F32),
        ],
        compiler_params=_params(2),
        name="dilated_attention",
    )(slopes, tables, kvq4, kvq4, kvq4)


def _out_proj_kernel(a_ref, x_ref, w_ref, o_ref):
    o_ref[...] = x_ref[...] + jnp.dot(a_ref[...].astype(_BF16), w_ref[...],
                                      preferred_element_type=_F32)


def _out_proj(att, x, w, *, tm=512):
    n, d = x.shape
    return pl.pallas_call(
        _out_proj_kernel,
        grid=(n // tm,),
        in_specs=[
            pl.BlockSpec((tm, d), lambda i: (i, 0)),
            pl.BlockSpec((tm, d), lambda i: (i, 0)),
            pl.BlockSpec((d, d), lambda i: (0, 0)),
        ],
        out_specs=pl.BlockSpec((tm, d), lambda i: (i, 0)),
        out_shape=jax.ShapeDtypeStruct((n, d), _F32),
        compiler_params=_params(1),
        name="out_proj",
    )(att, x, w)


def kernel(x, a_norm_g, conv_w1, conv_b1, conv_dw, conv_dw_b, conv_ln_g, conv_ln_b, conv_w2, conv_b2,
           kv_norm_g, w_k, w_v, b_norm_g, w_q, w_o, ffn_norm_g, ffn_w_gate, ffn_w_up, ffn_w_down,
           final_norm_g):
    batch, seq, d = x.shape
    n_a = a_norm_g.shape[0]
    n_b = b_norm_g.shape[0]
    depth = n_a + n_b
    assert d % N_HEADS == 0 and d // N_HEADS == LANES
    assert BRANCH_DILATIONS == (1, 4, 16) and seq % (max(BRANCH_DILATIONS) * Q_BLOCK * 2) == 0

    row = lambda v: v.reshape(1, -1)
    h = x.reshape(batch * seq, d)
    head_dim = d // N_HEADS
    slopes = jnp.exp2(-8.0 * jnp.arange(1, N_HEADS + 1, dtype=_F32) / N_HEADS)
    tables = jnp.asarray(_step_tables())
    kvq = None

    for layer in range(depth):
        if layer < n_a:
            a = layer
            w1 = _cast_stack([(conv_w1, a)])[0]
            w2 = _cast_stack([(conv_w2, a)])[0]
            u = _glu_in(h, row(a_norm_g[a]), w1, row(conv_b1[a]))
            h = _conv_out(u, h, conv_dw[a], row(conv_dw_b[a]), row(conv_ln_g[a]), row(conv_ln_b[a]),
                          w2, row(conv_b2[a]), seq=seq)
        else:
            i = layer - n_a
            assert n_b == 1, "one attention layer supported"
            w_kvq = _cast_stack([(w_k, None), (w_v, None), (w_q, i)])
            kvq = _kvq(h, row(kv_norm_g), row(b_norm_g[i]), w_kvq,
                       q_scale=head_dim ** -0.5 * LOG2E)
            att = _attention(kvq, slopes, tables, batch=batch, seq=seq)
            h = _out_proj(att.reshape(batch * seq, d), h, _cast_stack([(w_o, i)])[0])
        last = layer == depth - 1
        h = _ffn(h, row(ffn_norm_g[layer]),
                 _cast_stack([(ffn_w_gate, layer), (ffn_w_up, layer)]),
                 _cast_stack([(ffn_w_down, layer)]),
                 row(final_norm_g) if last else None)
    return h.reshape(batch, seq, d)
```

```python
import functools
import math

import numpy as np
import jax
import jax.numpy as jnp
from jax import lax
from jax.experimental import pallas as pl
from jax.experimental.pallas import tpu as pltpu

RMS_EPS = 1e-6
LN_EPS = 1e-5
N_HEADS = 16
CONV_WIDTH = 31
BRANCH_DILATIONS = (1, 4, 16)
WINDOW_STEPS = 128
Q_BLOCK = 128
GROUP = 4
MASKED = 1e30
LOG2E = math.log2(math.e)
LANES = 128

_VMEM_LIMIT = 56 * 1024 * 1024
_BF16 = jnp.bfloat16
_F32 = jnp.float32


def _params(n_axes):
    return pltpu.CompilerParams(dimension_semantics=("arbitrary",) * n_axes,
                                vmem_limit_bytes=_VMEM_LIMIT)


def _sigmoid(x):
    return 1.0 / (1.0 + jnp.exp(-x))


def _rmsnorm_rows(x_ref, out_refs, g_refs, chunk=256):
    n = x_ref.shape[0] // chunk

    def body(i, carry):
        r = pl.multiple_of(i * chunk, chunk)
        x = x_ref[pl.ds(r, chunk), :]
        xhat = x * lax.rsqrt(jnp.mean(x * x, axis=-1, keepdims=True) + RMS_EPS)
        for o_ref, g_ref in zip(out_refs, g_refs):
            o_ref[pl.ds(r, chunk), :] = (xhat * g_ref[...]).astype(o_ref.dtype)
        return carry

    lax.fori_loop(0, n, body, 0)


def _cast_kernel(*refs):
    *src_refs, o_ref = refs
    which = pl.program_id(0)
    for k, src_ref in enumerate(src_refs):
        @pl.when(which == k)
        def _(src_ref=src_ref):
            o_ref[...] = src_ref[...].astype(o_ref.dtype)


def _cast_stack(sources, *, block_elems=1 << 20):
    rows, cols = sources[0][0].shape[-2:]
    rb = 1 << ((block_elems // cols).bit_length() - 1)
    while rows % rb:
        rb //= 2
    assert rb >= 16
    nb = rows // rb

    def in_spec(k, layer):
        def block(which, b):
            return jnp.where(which == k, b, jnp.where(which < k, 0, nb - 1))
        if layer is None:
            return pl.BlockSpec((rb, cols), lambda which, b: (block(which, b), 0))
        return pl.BlockSpec((None, rb, cols), lambda which, b: (layer, block(which, b), 0))

    return pl.pallas_call(
        _cast_kernel,
        grid=(len(sources), nb),
        in_specs=[in_spec(k, layer) for k, (_, layer) in enumerate(sources)],
        out_specs=pl.BlockSpec((None, rb, cols), lambda which, b: (which, b, 0)),
        out_shape=jax.ShapeDtypeStruct((len(sources), rows, cols), _BF16),
        compiler_params=_params(2),
        name="cast_bf16",
    )(*[a for a, _ in sources])


def _glu_in_kernel(x_ref, g_ref, wa_ref, wg_ref, ba_ref, bg_ref, u_ref, xn_ref):
    @pl.when(pl.program_id(1) == 0)
    def _():
        _rmsnorm_rows(x_ref, [xn_ref], [g_ref])

    xn = xn_ref[...]
    a = jnp.dot(xn, wa_ref[...], preferred_element_type=_F32) + ba_ref[...]
    gate = jnp.dot(xn, wg_ref[...], preferred_element_type=_F32) + bg_ref[...]
    u_ref[...] = a * _sigmoid(gate)


def _glu_in(x, g, w1, b1, *, tm=1024, tn=1024):
    n, d = x.shape
    nj = d // tn
    return pl.pallas_call(
        _glu_in_kernel,
        grid=(n // tm, nj),
        in_specs=[
            pl.BlockSpec((tm, d), lambda i, j: (i, 0)),
            pl.BlockSpec((1, d), lambda i, j: (0, 0)),
            pl.BlockSpec((d, tn), lambda i, j: (0, j)),
            pl.BlockSpec((d, tn), lambda i, j: (0, j + nj)),
            pl.BlockSpec((1, tn), lambda i, j: (0, j)),
            pl.BlockSpec((1, tn), lambda i, j: (0, j + nj)),
        ],
        out_specs=pl.BlockSpec((tm, tn), lambda i, j: (i, j)),
        out_shape=jax.ShapeDtypeStruct((n, d), _F32),
        scratch_shapes=[pltpu.VMEM((tm, d), _BF16)],
        compiler_params=_params(2),
        name="glu_in",
    )(x, g, w1, w1, b1, b1)


_HALO = 32
_CONV_ROWS = 256


def _conv_out_kernel(u_ref, halo_ref, x_ref, dw_ref, dwb_ref, lng_ref, lnb_ref, w2_ref, b2_ref,
                     o_ref, buf_ref, y_ref, z_ref, *, blocks_per_seq):
    tm, d = u_ref.shape
    first = (pl.program_id(0) % blocks_per_seq) == 0
    for c in range(d // LANES):
        lanes = slice(c * LANES, (c + 1) * LANES)
        buf_ref[c, 0:_HALO, :] = jnp.where(first, 0.0, halo_ref[:, lanes])
        buf_ref[c, _HALO:, :] = u_ref[:, lanes]

    def conv_lanes(c, carry):
        lanes = pl.ds(pl.multiple_of(c * LANES, LANES), LANES)
        w = dw_ref[:, lanes]
        for r0 in range(0, tm, _CONV_ROWS):
            acc = jnp.broadcast_to(dwb_ref[:, lanes], (_CONV_ROWS, LANES))
            for k in range(CONV_WIDTH):
                shifted = buf_ref[c, pl.ds(r0 + _HALO - (CONV_WIDTH - 1) + k, _CONV_ROWS, stride=1), :]
                acc = acc + shifted * w[k:k + 1, :]
            y_ref[r0:r0 + _CONV_ROWS, lanes] = acc
        return carry

    lax.fori_loop(0, d // LANES, conv_lanes, 0)

    chunk = 256

    def norm_rows(i, carry):
        rows = pl.ds(pl.multiple_of(i * chunk, chunk), chunk)
        y = y_ref[rows, :]
        mu = jnp.mean(y, axis=-1, keepdims=True)
        yc = y - mu
        var = jnp.mean(yc * yc, axis=-1, keepdims=True)
        zn = yc * lax.rsqrt(var + LN_EPS) * lng_ref[...] + lnb_ref[...]
        z_ref[rows, :] = (zn * _sigmoid(zn)).astype(z_ref.dtype)
        return carry

    lax.fori_loop(0, tm // chunk, norm_rows, 0)

    o_ref[...] = (x_ref[...] + b2_ref[...]
                  + jnp.dot(z_ref[...], w2_ref[...], preferred_element_type=_F32))


def _conv_out(u, x, dw, dwb, lng, lnb, w2, b2, *, seq, tm=512):
    n, d = u.shape
    per_halo = tm // _HALO
    kern = functools.partial(_conv_out_kernel, blocks_per_seq=seq // tm)
    row = lambda i: (0, 0)
    return pl.pallas_call(
        kern,
        grid=(n // tm,),
        in_specs=[
            pl.BlockSpec((tm, d), lambda i: (i, 0)),
            pl.BlockSpec((_HALO, d), lambda i: (jnp.maximum(i * per_halo - 1, 0), 0)),
            pl.BlockSpec((tm, d), lambda i: (i, 0)),
            pl.BlockSpec((CONV_WIDTH, d), row),
            pl.BlockSpec((1, d), row),
            pl.BlockSpec((1, d), row),
            pl.BlockSpec((1, d), row),
            pl.BlockSpec((d, d), row, pipeline_mode=pl.Buffered(1)),
            pl.BlockSpec((1, d), row),
        ],
        out_specs=pl.BlockSpec((tm, d), lambda i: (i, 0)),
        out_shape=jax.ShapeDtypeStruct((n, d), _F32),
        scratch_shapes=[pltpu.VMEM((d // LANES, tm + _HALO, LANES), _F32),
                        pltpu.VMEM((tm, d), _F32),
                        pltpu.VMEM((tm, d), _BF16)],
        compiler_params=_params(1),
        name="conv_out",
    )(u, u, x, dw, dwb, lng, lnb, w2, b2)


def _ffn_kernel(x_ref, g_ref, wg_ref, wu_ref, wd_ref, *rest, final_norm):
    if final_norm:
        gf_ref, o_ref, xn_ref = rest
    else:
        o_ref, xn_ref = rest
    f = pl.program_id(1)

    @pl.when(f == 0)
    def _():
        _rmsnorm_rows(x_ref, [xn_ref], [g_ref])
        o_ref[...] = x_ref[...]

    xn = xn_ref[...]
    gate = jnp.dot(xn, wg_ref[...], preferred_element_type=_F32)
    up = jnp.dot(xn, wu_ref[...], preferred_element_type=_F32)
    act = (gate * _sigmoid(gate) * up).astype(_BF16)
    o_ref[...] += jnp.dot(act, wd_ref[...], preferred_element_type=_F32)

    if final_norm:
        @pl.when(f == pl.num_programs(1) - 1)
        def _():
            _rmsnorm_rows(o_ref, [o_ref], [gf_ref])


def _ffn(x, g, w_gate_up, w_down, gf=None, *, tm=1024, tf=512):
    n, d = x.shape
    f = w_gate_up.shape[2]
    final_norm = gf is not None
    in_specs = [
        pl.BlockSpec((tm, d), lambda i, j: (i, 0)),
        pl.BlockSpec((1, d), lambda i, j: (0, 0)),
        pl.BlockSpec((None, d, tf), lambda i, j: (0, 0, j)),
        pl.BlockSpec((None, d, tf), lambda i, j: (1, 0, j)),
        pl.BlockSpec((None, tf, d), lambda i, j: (0, j, 0)),
    ]
    args = [x, g, w_gate_up, w_gate_up, w_down]
    if final_norm:
        in_specs.append(pl.BlockSpec((1, d), lambda i, j: (0, 0)))
        args.append(gf)
    return pl.pallas_call(
        functools.partial(_ffn_kernel, final_norm=final_norm),
        grid=(n // tm, f // tf),
        in_specs=in_specs,
        out_specs=pl.BlockSpec((tm, d), lambda i, j: (i, 0)),
        out_shape=jax.ShapeDtypeStruct((n, d), _F32),
        scratch_shapes=[pltpu.VMEM((tm, d), _BF16)],
        compiler_params=_params(2),
        name="ffn_final" if final_norm else "ffn",
    )(*args)


def _kvq_kernel(x_ref, gkv_ref, gq_ref, w_ref, o_ref, xkv_ref, xq_ref, *, q_scale, blocks_per_proj):
    j = pl.program_id(1)

    @pl.when(j == 0)
    def _():
        _rmsnorm_rows(x_ref, [xkv_ref], [gkv_ref])
        _rmsnorm_rows(x_ref, [xq_ref], [gq_ref])

    @pl.when(j < 2 * blocks_per_proj)
    def _():
        o_ref[...] = jnp.dot(xkv_ref[...], w_ref[...], preferred_element_type=_F32)

    @pl.when(j >= 2 * blocks_per_proj)
    def _():
        o_ref[...] = jnp.dot(xq_ref[...], w_ref[...], preferred_element_type=_F32) * q_scale


def _kvq(x, gkv, gq, w_kvq, *, q_scale, tm=1024, tn=1024):
    n, d = x.shape
    per = d // tn
    return pl.pallas_call(
        functools.partial(_kvq_kernel, q_scale=q_scale, blocks_per_proj=per),
        grid=(n // tm, 3 * per),
        in_specs=[
            pl.BlockSpec((tm, d), lambda i, j: (i, 0)),
            pl.BlockSpec((1, d), lambda i, j: (0, 0)),
            pl.BlockSpec((1, d), lambda i, j: (0, 0)),
            pl.BlockSpec((None, d, tn), lambda i, j: (j // per, 0, j % per)),
        ],
        out_specs=pl.BlockSpec((None, tm, tn), lambda i, j: (j // per, i, j % per)),
        out_shape=jax.ShapeDtypeStruct((3, n, d), _F32),
        scratch_shapes=[pltpu.VMEM((tm, d), _BF16), pltpu.VMEM((tm, d), _BF16)],
        compiler_params=_params(2),
        name="kvq",
    )(x, gkv, gq, w_kvq)


def _step_tables():
    p = np.arange(Q_BLOCK)[:, None]
    c = np.arange(2 * Q_BLOCK)[None, :]
    pair = []
    for shift in (0, Q_BLOCK):
        j = shift + p - c
        valid = (j >= 0) & (j <= WINDOW_STEPS)
        pair.append(np.where(valid, j.astype(np.float32), np.float32(MASKED)))
    return np.stack(pair).astype(np.float32)


def _attention_kernel(slopes_ref, tab_ref, q_ref, k_ref, v_ref, o_ref,
                      bias_ref, qs_ref, ks_ref, vs_ref, t4_ref, s_ref, e_ref, rl_ref, ob_ref, lse_ref):
    seq = q_ref.shape[0]
    blocks_per_branch = seq // Q_BLOCK
    n_br = len(BRANCH_DILATIONS)
    n_groups = n_br * blocks_per_branch // GROUP
    assert blocks_per_branch % GROUP == 0 and n_groups % 2 == 0 and n_groups >= 4
    h = pl.program_id(1)
    neg_slope2 = -slopes_ref[h] * LOG2E
    for b, dil in enumerate(BRANCH_DILATIONS):
        for t in range(2):
            bias_ref[b, t] = tab_ref[t] * (dil * neg_slope2)

    chunk = seq // 16
    quarter = seq // 4

    def finer(c):
        return pl.ds((c // 4) + 4 * (c % 4) * chunk, chunk, stride=4)

    def finer16(c):
        return pl.ds((c % 4) * quarter + c // 4, chunk, stride=4)

    srcs = (q_ref, k_ref, v_ref)
    dsts = (qs_ref, ks_ref, vs_ref)

    def regroup4(c, carry):
        dst = pl.ds(pl.multiple_of(c * chunk, chunk), chunk)
        for a in range(3):
            dsts[a][0, dst, :] = srcs[a][dst, :].astype(_BF16)
            by4 = srcs[a][finer(c), :]
            t4_ref[a, dst, :] = by4
            dsts[a][1, dst, :] = by4.astype(_BF16)
        return carry

    lax.fori_loop(0, seq // chunk, regroup4, 0)

    def regroup16(c, carry):
        dst = pl.ds(pl.multiple_of(c * chunk, chunk), chunk)
        for a in range(3):
            dsts[a][2, dst, :] = t4_ref[a, finer16(c), :].astype(_BF16)
        return carry

    lax.fori_loop(0, seq // chunk, regroup16, 0)

    def locate(g, u):
        t = g * GROUP + u
        b = t // blocks_per_branch
        jb = t % blocks_per_branch
        per_class = lax.shift_right_logical(blocks_per_branch, 2 * b)
        is_first = (jb & (per_class - 1)) == 0
        q0 = pl.multiple_of(jb * Q_BLOCK, Q_BLOCK)
        k0 = pl.multiple_of(jnp.where(is_first, q0, q0 - Q_BLOCK), Q_BLOCK)
        return b, q0, k0, jnp.where(is_first, 0, 1)

    def scores(g, slot):
        for u in range(GROUP):
            b, q0, k0, _ = locate(g, u)
            q = qs_ref[b, pl.ds(q0, Q_BLOCK), :]
            k = ks_ref[b, pl.ds(k0, 2 * Q_BLOCK), :]
            s_ref[slot, u] = lax.dot_general(q, k, (((1,), (1,)), ((), ())),
                                             preferred_element_type=_F32)

    def softmax(g, slot):
        for u in range(GROUP):
            b, q0, _, tab = locate(g, u)
            s = s_ref[slot, u] + bias_ref[b, tab]
            m = jnp.max(s, axis=-1, keepdims=True)
            e = jnp.exp2(s - m)
            l = jnp.sum(e, axis=-1, keepdims=True)
            e_ref[slot, u] = e.astype(_BF16)
            rl_ref[slot, u] = jnp.broadcast_to(1.0 / l, (Q_BLOCK, LANES))
            lse_ref[b, pl.ds(q0, Q_BLOCK), :] = jnp.broadcast_to(m + jnp.log2(l), (Q_BLOCK, LANES))

    def values(g, slot):
        for u in range(GROUP):
            b, q0, k0, _ = locate(g, u)
            v = vs_ref[b, pl.ds(k0, 2 * Q_BLOCK), :]
            acc = jnp.dot(e_ref[slot, u], v, preferred_element_type=_F32)
            ob_ref[b, pl.ds(q0, Q_BLOCK), :] = acc * rl_ref[slot, u]

    zero = jnp.int32(0)
    scores(zero, 0)
    softmax(zero, 0)
    scores(zero + 1, 1)

    def trip(i, carry):
        g = 2 * i
        values(g - 2, 0)
        softmax(g - 1, 1)
        scores(g, 0)
        values(g - 1, 1)
        softmax(g, 0)
        scores(g + 1, 1)
        return carry

    lax.fori_loop(1, n_groups // 2, trip, 0)
    values(zero + (n_groups - 2), 0)
    softmax(zero + (n_groups - 1), 1)
    values(zero + (n_groups - 1), 1)

    def coarsen(c, carry):
        src = pl.ds(pl.multiple_of(c * chunk, chunk), chunk)
        t4_ref[0, finer16(c), :] = ob_ref[2, src, :]
        t4_ref[1, finer16(c), :] = lse_ref[2, src, :]
        return carry

    lax.fori_loop(0, seq // chunk, coarsen, 0)

    rows_per = 256

    def merge(c, carry):
        by4 = pl.ds(pl.multiple_of(c * rows_per, rows_per), rows_per)
        per_class = quarter // rows_per
        natural = pl.ds((c // per_class) + 4 * (c % per_class) * rows_per, rows_per, stride=4)
        ls = [lse_ref[0, natural, :], lse_ref[1, by4, :], t4_ref[1, by4, :]]
        os = [ob_ref[0, natural, :], ob_ref[1, by4, :], t4_ref[0, by4, :]]
        l_max = functools.reduce(jnp.maximum, ls)
        num = 0.0
        den = 0.0
        for b in range(n_br):
            w = jnp.exp2(ls[b] - l_max)
            num = num + w * os[b]
            den = den + w
        o_ref[natural, :] = num / den
        return carry

    lax.fori_loop(0, seq // rows_per, merge, 0)


def _attention(kvq, slopes, tables, *, batch, seq):
    _, n, d = kvq.shape
    e = d // N_HEADS
    n_br = len(BRANCH_DILATIONS)
    kvq4 = kvq.reshape(3, batch, seq, d)

    def qkv_spec(which):
        return pl.BlockSpec((None, None, seq, e), lambda b, h: (which, b, 0, h))

    return pl.pallas_call(
        _attention_kernel,
        grid=(batch, N_HEADS),
        in_specs=[
            pl.BlockSpec(memory_space=pltpu.SMEM),
            pl.BlockSpec(tables.shape, lambda b, h: (0, 0, 0)),
            qkv_spec(2), qkv_spec(0), qkv_spec(1),
        ],
        out_specs=pl.BlockSpec((None, seq, e), lambda b, h: (b, 0, h)),
        out_shape=jax.ShapeDtypeStruct((batch, seq, d), _F32),
        scratch_shapes=[
            pltpu.VMEM((n_br, 2, Q_BLOCK, 2 * Q_BLOCK), _F32),
            pltpu.VMEM((n_br, seq, e), _BF16),
            pltpu.VMEM((n_br, seq, e), _BF16),
            pltpu.VMEM((n_br, seq, e), _BF16),
            pltpu.VMEM((3, seq, e), _F32),
            pltpu.VMEM((2, GROUP, Q_BLOCK, 2 * Q_BLOCK), _F32),
            pltpu.VMEM((2, GROUP, Q_BLOCK, 2 * Q_BLOCK), _BF16),
            pltpu.VMEM((2, GROUP, Q_BLOCK, LANES), _F32),
            pltpu.VMEM((n_br, seq, e), _F32),
            pltpu.VMEM((n_br, seq, LANES), _F32),
        ],
        compiler_params=_params(2),
        name="dilated_attention",
    )(slopes, tables, kvq4, kvq4, kvq4)


def _out_proj_kernel(a_ref, x_ref, w_ref, o_ref):
    o_ref[...] = x_ref[...] + jnp.dot(a_ref[...].astype(_BF16), w_ref[...],
                                      preferred_element_type=_F32)


def _out_proj(att, x, w, *, tm=512):
    n, d = x.shape
    return pl.pallas_call(
        _out_proj_kernel,
        grid=(n // tm,),
        in_specs=[
            pl.BlockSpec((tm, d), lambda i: (i, 0)),
            pl.BlockSpec((tm, d), lambda i: (i, 0)),
            pl.BlockSpec((d, d), lambda i: (0, 0)),
        ],
        out_specs=pl.BlockSpec((tm, d), lambda i: (i, 0)),
        out_shape=jax.ShapeDtypeStruct((n, d), _F32),
        compiler_params=_params(1),
        name="out_proj",
    )(att, x, w)


def kernel(x, a_norm_g, conv_w1, conv_b1, conv_dw, conv_dw_b, conv_ln_g, conv_ln_b, conv_w2, conv_b2,
           kv_norm_g, w_k, w_v, b_norm_g, w_q, w_o, ffn_norm_g, ffn_w_gate, ffn_w_up, ffn_w_down,
           final_norm_g):
    batch, seq, d = x.shape
    n_a = a_norm_g.shape[0]
    n_b = b_norm_g.shape[0]
    depth = n_a + n_b
    assert d % N_HEADS == 0 and d // N_HEADS == LANES
    assert BRANCH_DILATIONS == (1, 4, 16) and seq % (max(BRANCH_DILATIONS) * Q_BLOCK * 2) == 0

    row = lambda v: v.reshape(1, -1)
    h = x.reshape(batch * seq, d)
    head_dim = d // N_HEADS
    slopes = jnp.exp2(-8.0 * jnp.arange(1, N_HEADS + 1, dtype=_F32) / N_HEADS)
    tables = jnp.asarray(_step_tables())
    kvq = None

    for layer in range(depth):
        if layer < n_a:
            a = layer
            w1 = _cast_stack([(conv_w1, a)])[0]
            w2 = _cast_stack([(conv_w2, a)])[0]
            u = _glu_in(h, row(a_norm_g[a]), w1, row(conv_b1[a]))
            h = _conv_out(u, h, conv_dw[a], row(conv_dw_b[a]), row(conv_ln_g[a]), row(conv_ln_b[a]),
                          w2, row(conv_b2[a]), seq=seq)
        else:
            i = layer - n_a
            assert n_b == 1, "one attention layer supported"
            w_kvq = _cast_stack([(w_k, None), (w_v, None), (w_q, i)])
            kvq = _kvq(h, row(kv_norm_g), row(b_norm_g[i]), w_kvq,
                       q_scale=head_dim ** -0.5 * LOG2E)
            att = _attention(kvq, slopes, tables, batch=batch, seq=seq)
            h = _out_proj(att.reshape(batch * seq, d), h, _cast_stack([(w_o, i)])[0])
        last = layer == depth - 1
        h = _ffn(h, row(ffn_norm_g[layer]),
                 _cast_stack([(ffn_w_gate, layer), (ffn_w_up, layer)]),
                 _cast_stack([(ffn_w_down, layer)]),
                 row(final_norm_g) if last else None)
    return h.reshape(batch, seq, d)
```

```python
import functools
import math

import numpy as np
import jax
import jax.numpy as jnp
from jax import lax
from jax.experimental import pallas as pl
from jax.experimental.pallas import tpu as pltpu

RMS_EPS = 1e-6
LN_EPS = 1e-5
N_HEADS = 16
CONV_WIDTH = 31
BRANCH_DILATIONS = (1, 4, 16)
DILATION_RATIO = 4
WINDOW_STEPS = 128
Q_BLOCK = 128
GROUP = 4
MASKED = 1e30
LOG2E = math.log2(math.e)
LANES = 128

_VMEM_LIMIT = 56 * 1024 * 1024
_BF16 = jnp.bfloat16
_F32 = jnp.float32


def _params(n_axes):
    return pltpu.CompilerParams(dimension_semantics=("arbitrary",) * n_axes,
                                vmem_limit_bytes=_VMEM_LIMIT)


def _sigmoid(x):
    return 1.0 / (1.0 + jnp.exp(-x))


def _rmsnorm_rows(x_ref, out_refs, g_refs, chunk=256):
    n = x_ref.shape[0] // chunk

    def body(i, carry):
        r = pl.multiple_of(i * chunk, chunk)
        x = x_ref[pl.ds(r, chunk), :]
        xhat = x * lax.rsqrt(jnp.mean(x * x, axis=-1, keepdims=True) + RMS_EPS)
        for o_ref, g_ref in zip(out_refs, g_refs):
            o_ref[pl.ds(r, chunk), :] = (xhat * g_ref[...]).astype(o_ref.dtype)
        return carry

    lax.fori_loop(0, n, body, 0)


def _cast_kernel(*refs):
    *src_refs, o_ref = refs
    which = pl.program_id(0)
    for k, src_ref in enumerate(src_refs):
        @pl.when(which == k)
        def _(src_ref=src_ref):
            o_ref[...] = src_ref[...].astype(o_ref.dtype)


def _cast_stack(sources, *, block_elems=1 << 20):
    rows, cols = sources[0][0].shape[-2:]
    rb = 1 << ((block_elems // cols).bit_length() - 1)
    while rows % rb:
        rb //= 2
    assert rb >= 16
    nb = rows // rb

    def in_spec(k, layer):
        def block(which, b):
            return jnp.where(which == k, b, jnp.where(which < k, 0, nb - 1))
        if layer is None:
            return pl.BlockSpec((rb, cols), lambda which, b: (block(which, b), 0))
        return pl.BlockSpec((None, rb, cols), lambda which, b: (layer, block(which, b), 0))

    return pl.pallas_call(
        _cast_kernel,
        grid=(len(sources), nb),
        in_specs=[in_spec(k, layer) for k, (_, layer) in enumerate(sources)],
        out_specs=pl.BlockSpec((None, rb, cols), lambda which, b: (which, b, 0)),
        out_shape=jax.ShapeDtypeStruct((len(sources), rows, cols), _BF16),
        compiler_params=_params(2),
        name="cast_bf16",
    )(*[a for a, _ in sources])


def _glu_in_kernel(x_ref, g_ref, wa_ref, wg_ref, ba_ref, bg_ref, u_ref, xn_ref):
    @pl.when(pl.program_id(1) == 0)
    def _():
        _rmsnorm_rows(x_ref, [xn_ref], [g_ref])

    xn = xn_ref[...]
    a = jnp.dot(xn, wa_ref[...], preferred_element_type=_F32) + ba_ref[...]
    gate = jnp.dot(xn, wg_ref[...], preferred_element_type=_F32) + bg_ref[...]
    u_ref[...] = a * _sigmoid(gate)


def _glu_in(x, g, w1, b1, *, tm=1024, tn=1024):
    n, d = x.shape
    nj = d // tn
    return pl.pallas_call(
        _glu_in_kernel,
        grid=(n // tm, nj),
        in_specs=[
            pl.BlockSpec((tm, d), lambda i, j: (i, 0)),
            pl.BlockSpec((1, d), lambda i, j: (0, 0)),
            pl.BlockSpec((d, tn), lambda i, j: (0, j)),
            pl.BlockSpec((d, tn), lambda i, j: (0, j + nj)),
            pl.BlockSpec((1, tn), lambda i, j: (0, j)),
            pl.BlockSpec((1, tn), lambda i, j: (0, j + nj)),
        ],
        out_specs=pl.BlockSpec((tm, tn), lambda i, j: (i, j)),
        out_shape=jax.ShapeDtypeStruct((n, d), _F32),
        scratch_shapes=[pltpu.VMEM((tm, d), _BF16)],
        compiler_params=_params(2),
        name="glu_in",
    )(x, g, w1, w1, b1, b1)


_HALO = 32
_CONV_ROWS = 256


def _conv_out_kernel(u_ref, halo_ref, x_ref, dw_ref, dwb_ref, lng_ref, lnb_ref, w2_ref, b2_ref,
                     o_ref, buf_ref, y_ref, z_ref, *, blocks_per_seq):
    tm, d = u_ref.shape
    first = (pl.program_id(0) % blocks_per_seq) == 0
    for c in range(d // LANES):
        lanes = slice(c * LANES, (c + 1) * LANES)
        buf_ref[c, 0:_HALO, :] = jnp.where(first, 0.0, halo_ref[:, lanes])
        buf_ref[c, _HALO:, :] = u_ref[:, lanes]

    def conv_lanes(c, carry):
        lanes = pl.ds(pl.multiple_of(c * LANES, LANES), LANES)
        w = dw_ref[:, lanes]
        for r0 in range(0, tm, _CONV_ROWS):
            acc = jnp.broadcast_to(dwb_ref[:, lanes], (_CONV_ROWS, LANES))
            for k in range(CONV_WIDTH):
                shifted = buf_ref[c, pl.ds(r0 + _HALO - (CONV_WIDTH - 1) + k, _CONV_ROWS, stride=1), :]
                acc = acc + shifted * w[k:k + 1, :]
            y_ref[r0:r0 + _CONV_ROWS, lanes] = acc
        return carry

    lax.fori_loop(0, d // LANES, conv_lanes, 0)

    chunk = 256

    def norm_rows(i, carry):
        rows = pl.ds(pl.multiple_of(i * chunk, chunk), chunk)
        y = y_ref[rows, :]
        mu = jnp.mean(y, axis=-1, keepdims=True)
        yc = y - mu
        var = jnp.mean(yc * yc, axis=-1, keepdims=True)
        zn = yc * lax.rsqrt(var + LN_EPS) * lng_ref[...] + lnb_ref[...]
        z_ref[rows, :] = (zn * _sigmoid(zn)).astype(z_ref.dtype)
        return carry

    lax.fori_loop(0, tm // chunk, norm_rows, 0)

    o_ref[...] = (x_ref[...] + b2_ref[...]
                  + jnp.dot(z_ref[...], w2_ref[...], preferred_element_type=_F32))


def _conv_out(u, x, dw, dwb, lng, lnb, w2, b2, *, seq, tm=512):
    n, d = u.shape
    per_halo = tm // _HALO
    kern = functools.partial(_conv_out_kernel, blocks_per_seq=seq // tm)
    row = lambda i: (0, 0)
    return pl.pallas_call(
        kern,
        grid=(n // tm,),
        in_specs=[
            pl.BlockSpec((tm, d), lambda i: (i, 0)),
            pl.BlockSpec((_HALO, d), lambda i: (jnp.maximum(i * per_halo - 1, 0), 0)),
            pl.BlockSpec((tm, d), lambda i: (i, 0)),
            pl.BlockSpec((CONV_WIDTH, d), row),
            pl.BlockSpec((1, d), row),
            pl.BlockSpec((1, d), row),
            pl.BlockSpec((1, d), row),
            pl.BlockSpec((d, d), row, pipeline_mode=pl.Buffered(1)),
            pl.BlockSpec((1, d), row),
        ],
        out_specs=pl.BlockSpec((tm, d), lambda i: (i, 0)),
        out_shape=jax.ShapeDtypeStruct((n, d), _F32),
        scratch_shapes=[pltpu.VMEM((d // LANES, tm + _HALO, LANES), _F32),
                        pltpu.VMEM((tm, d), _F32),
                        pltpu.VMEM((tm, d), _BF16)],
        compiler_params=_params(1),
        name="conv_out",
    )(u, u, x, dw, dwb, lng, lnb, w2, b2)


def _ffn_kernel(x_ref, g_ref, wgu_hbm, wd_hbm, *rest, final_norm, tf):
    if final_norm:
        gf_ref, o_ref, xn_ref, wg_buf, wu_buf, wd_buf, sem = rest
    else:
        o_ref, xn_ref, wg_buf, wu_buf, wd_buf, sem = rest
    n_chunks = wd_hbm.shape[1] // tf
    i = pl.program_id(0)
    n_tiles = pl.num_programs(0)

    def chunk_copies(j, slot):
        cols = pl.ds(pl.multiple_of(j * tf, tf), tf)
        return (pltpu.make_async_copy(wgu_hbm.at[0, :, cols], wg_buf.at[slot], sem.at[0, slot]),
                pltpu.make_async_copy(wgu_hbm.at[1, :, cols], wu_buf.at[slot], sem.at[1, slot]),
                pltpu.make_async_copy(wd_hbm.at[0, cols, :], wd_buf.at[slot], sem.at[2, slot]))

    @pl.when(i == 0)
    def _():
        for copy in chunk_copies(0, 0):
            copy.start()

    _rmsnorm_rows(x_ref, [xn_ref], [g_ref])
    o_ref[...] = x_ref[...]

    def body(j, carry):
        count = i * n_chunks + j
        slot = count % 2
        is_last = jnp.logical_and(i == n_tiles - 1, j == n_chunks - 1)

        @pl.when(jnp.logical_not(is_last))
        def _():
            for copy in chunk_copies((j + 1) % n_chunks, 1 - slot):
                copy.start()

        for copy in chunk_copies(j, slot):
            copy.wait()
        xn = xn_ref[...]
        gate = jnp.dot(xn, wg_buf[slot], preferred_element_type=_F32)
        up = jnp.dot(xn, wu_buf[slot], preferred_element_type=_F32)
        act = (gate * _sigmoid(gate) * up).astype(_BF16)
        o_ref[...] += jnp.dot(act, wd_buf[slot], preferred_element_type=_F32)
        return carry

    lax.fori_loop(0, n_chunks, body, 0)

    if final_norm:
        _rmsnorm_rows(o_ref, [o_ref], [gf_ref])


def _ffn(x, g, w_gate_up, w_down, gf=None, *, tm=1024, tf=512):
    n, d = x.shape
    f = w_gate_up.shape[2]
    assert f % tf == 0
    final_norm = gf is not None
    in_specs = [
        pl.BlockSpec((tm, d), lambda i: (i, 0)),
        pl.BlockSpec((1, d), lambda i: (0, 0)),
        pl.BlockSpec(memory_space=pl.ANY),
        pl.BlockSpec(memory_space=pl.ANY),
    ]
    args = [x, g, w_gate_up, w_down]
    if final_norm:
        in_specs.append(pl.BlockSpec((1, d), lambda i: (0, 0)))
        args.append(gf)
    return pl.pallas_call(
        functools.partial(_ffn_kernel, final_norm=final_norm, tf=tf),
        grid=(n // tm,),
        in_specs=in_specs,
        out_specs=pl.BlockSpec((tm, d), lambda i: (i, 0)),
        out_shape=jax.ShapeDtypeStruct((n, d), _F32),
        scratch_shapes=[pltpu.VMEM((tm, d), _BF16),
                        pltpu.VMEM((2, d, tf), _BF16),
                        pltpu.VMEM((2, d, tf), _BF16),
                        pltpu.VMEM((2, tf, d), _BF16),
                        pltpu.SemaphoreType.DMA((3, 2))],
        compiler_params=_params(1),
        name="ffn_final" if final_norm else "ffn",
    )(*args)


def _kvq_kernel(x_ref, gkv_ref, gq_ref, w_ref, o_ref, xkv_ref, xq_ref, *, q_scale, blocks_per_proj):
    j = pl.program_id(1)

    @pl.when(j == 0)
    def _():
        _rmsnorm_rows(x_ref, [xkv_ref], [gkv_ref])
        _rmsnorm_rows(x_ref, [xq_ref], [gq_ref])

    @pl.when(j < 2 * blocks_per_proj)
    def _():
        o_ref[...] = jnp.dot(xkv_ref[...], w_ref[...], preferred_element_type=_F32)

    @pl.when(j >= 2 * blocks_per_proj)
    def _():
        o_ref[...] = jnp.dot(xq_ref[...], w_ref[...], preferred_element_type=_F32) * q_scale


def _kvq(x, gkv, gq, w_kvq, *, q_scale, tm=1024, tn=1024):
    n, d = x.shape
    per = d // tn
    return pl.pallas_call(
        functools.partial(_kvq_kernel, q_scale=q_scale, blocks_per_proj=per),
        grid=(n // tm, 3 * per),
        in_specs=[
            pl.BlockSpec((tm, d), lambda i, j: (i, 0)),
            pl.BlockSpec((1, d), lambda i, j: (0, 0)),
            pl.BlockSpec((1, d), lambda i, j: (0, 0)),
            pl.BlockSpec((None, d, tn), lambda i, j: (j // per, 0, j % per)),
        ],
        out_specs=pl.BlockSpec((None, tm, tn), lambda i, j: (j // per, i, j % per)),
        out_shape=jax.ShapeDtypeStruct((3, n, d), _F32),
        scratch_shapes=[pltpu.VMEM((tm, d), _BF16), pltpu.VMEM((tm, d), _BF16)],
        compiler_params=_params(2),
        name="kvq",
    )(x, gkv, gq, w_kvq)


def _step_tables():
    p = np.arange(Q_BLOCK)[:, None]
    c = np.arange(2 * Q_BLOCK)[None, :]
    pair = []
    for shift in (0, Q_BLOCK):
        j = shift + p - c
        valid = (j >= 0) & (j <= WINDOW_STEPS)
        pair.append(np.where(valid, j.astype(np.float32), np.float32(MASKED)))
    return np.stack(pair).astype(np.float32)


def _attention_kernel(slopes_ref, tab_ref, q_ref, k_ref, v_ref, o_ref,
                      bias_ref, qs_ref, ks_ref, vs_ref, t4_ref, s_ref, e_ref, rl_ref, ob_ref, lse_ref):
    seq = q_ref.shape[0]
    blocks_per_branch = seq // Q_BLOCK
    n_br = len(BRANCH_DILATIONS)
    n_groups = n_br * blocks_per_branch // GROUP
    assert blocks_per_branch % GROUP == 0 and n_groups % 2 == 0 and n_groups >= 4
    h = pl.program_id(1)
    neg_slope2 = -slopes_ref[h] * LOG2E
    for b, dil in enumerate(BRANCH_DILATIONS):
        for t in range(2):
            bias_ref[b, t] = tab_ref[t] * (dil * neg_slope2)

    ratio = DILATION_RATIO
    chunk = seq // max(BRANCH_DILATIONS)
    quarter = seq // ratio

    def finer(c):
        return pl.ds((c // ratio) + ratio * (c % ratio) * chunk, chunk, stride=ratio)

    def finer16(c):
        return pl.ds((c % ratio) * quarter + c // ratio, chunk, stride=ratio)

    srcs = (q_ref, k_ref, v_ref)
    dsts = (qs_ref, ks_ref, vs_ref)

    def regroup4(c, carry):
        dst = pl.ds(pl.multiple_of(c * chunk, chunk), chunk)
        for a in range(3):
            dsts[a][0, dst, :] = srcs[a][dst, :].astype(_BF16)
            by4 = srcs[a][finer(c), :]
            t4_ref[a, dst, :] = by4
            dsts[a][1, dst, :] = by4.astype(_BF16)
        return carry

    lax.fori_loop(0, seq // chunk, regroup4, 0)

    def regroup16(c, carry):
        dst = pl.ds(pl.multiple_of(c * chunk, chunk), chunk)
        for a in range(3):
            dsts[a][2, dst, :] = t4_ref[a, finer16(c), :].astype(_BF16)
        return carry

    lax.fori_loop(0, seq // chunk, regroup16, 0)

    def locate(g, u):
        t = g * GROUP + u
        b = t // blocks_per_branch
        jb = t % blocks_per_branch
        per_class = lax.shift_right_logical(blocks_per_branch, 2 * b)
        is_first = (jb & (per_class - 1)) == 0
        q0 = pl.multiple_of(jb * Q_BLOCK, Q_BLOCK)
        k0 = pl.multiple_of(jnp.where(is_first, q0, q0 - Q_BLOCK), Q_BLOCK)
        return b, q0, k0, jnp.where(is_first, 0, 1)

    def scores(g, slot):
        for u in range(GROUP):
            b, q0, k0, _ = locate(g, u)
            q = qs_ref[b, pl.ds(q0, Q_BLOCK), :]
            k = ks_ref[b, pl.ds(k0, 2 * Q_BLOCK), :]
            s_ref[slot, u] = lax.dot_general(q, k, (((1,), (1,)), ((), ())),
                                             preferred_element_type=_F32)

    def softmax(g, slot):
        for u in range(GROUP):
            b, q0, _, tab = locate(g, u)
            s = s_ref[slot, u] + bias_ref[b, tab]
            m = jnp.max(s, axis=-1, keepdims=True)
            e = jnp.exp2(s - m)
            l = jnp.sum(e, axis=-1, keepdims=True)
            e_ref[slot, u] = e.astype(_BF16)
            rl_ref[slot, u] = jnp.broadcast_to(1.0 / l, (Q_BLOCK, LANES))
            lse_ref[b, pl.ds(q0, Q_BLOCK), :] = jnp.broadcast_to(m + jnp.log2(l), (Q_BLOCK, LANES))

    def values(g, slot):
        for u in range(GROUP):
            b, q0, k0, _ = locate(g, u)
            v = vs_ref[b, pl.ds(k0, 2 * Q_BLOCK), :]
            acc = jnp.dot(e_ref[slot, u], v, preferred_element_type=_F32)
            ob_ref[b, pl.ds(q0, Q_BLOCK), :] = acc * rl_ref[slot, u]

    zero = jnp.int32(0)
    scores(zero, 0)
    softmax(zero, 0)
    scores(zero + 1, 1)

    def trip(i, carry):
        g = 2 * i
        values(g - 2, 0)
        softmax(g - 1, 1)
        scores(g, 0)
        values(g - 1, 1)
        softmax(g, 0)
        scores(g + 1, 1)
        return carry

    lax.fori_loop(1, n_groups // 2, trip, 0)
    values(zero + (n_groups - 2), 0)
    softmax(zero + (n_groups - 1), 1)
    values(zero + (n_groups - 1), 1)

    def coarsen(c, carry):
        src = pl.ds(pl.multiple_of(c * chunk, chunk), chunk)
        t4_ref[0, finer16(c), :] = ob_ref[2, src, :]
        t4_ref[1, finer16(c), :] = lse_ref[2, src, :]
        return carry

    lax.fori_loop(0, seq // chunk, coarsen, 0)

    rows_per = 256

    def merge(c, carry):
        by4 = pl.ds(pl.multiple_of(c * rows_per, rows_per), rows_per)
        per_class = quarter // rows_per
        natural = pl.ds((c // per_class) + ratio * (c % per_class) * rows_per, rows_per, stride=ratio)
        ls = [lse_ref[0, natural, :], lse_ref[1, by4, :], t4_ref[1, by4, :]]
        os = [ob_ref[0, natural, :], ob_ref[1, by4, :], t4_ref[0, by4, :]]
        l_max = functools.reduce(jnp.maximum, ls)
        num = 0.0
        den = 0.0
        for b in range(n_br):
            w = jnp.exp2(ls[b] - l_max)
            num = num + w * os[b]
            den = den + w
        o_ref[natural, :] = num / den
        return carry

    lax.fori_loop(0, seq // rows_per, merge, 0)


def _attention(kvq, slopes, tables, *, batch, seq):
    _, n, d = kvq.shape
    e = d // N_HEADS
    n_br = len(BRANCH_DILATIONS)
    kvq4 = kvq.reshape(3, batch, seq, d)

    def qkv_spec(which):
        return pl.BlockSpec((None, None, seq, e), lambda b, h: (which, b, 0, h))

    return pl.pallas_call(
        _attention_kernel,
        grid=(batch, N_HEADS),
        in_specs=[
            pl.BlockSpec(memory_space=pltpu.SMEM),
            pl.BlockSpec(tables.shape, lambda b, h: (0, 0, 0)),
            qkv_spec(2), qkv_spec(0), qkv_spec(1),
        ],
        out_specs=pl.BlockSpec((None, seq, e), lambda b, h: (b, 0, h)),
        out_shape=jax.ShapeDtypeStruct((batch, seq, d), _F32),
        scratch_shapes=[
            pltpu.VMEM((n_br, 2, Q_BLOCK, 2 * Q_BLOCK), _F32),
            pltpu.VMEM((n_br, seq, e), _BF16),
            pltpu.VMEM((n_br, seq, e), _BF16),
            pltpu.VMEM((n_br, seq, e), _BF16),
            pltpu.VMEM((3, seq, e), _F32),
            pltpu.VMEM((2, GROUP, Q_BLOCK, 2 * Q_BLOCK), _F32),
            pltpu.VMEM((2, GROUP, Q_BLOCK, 2 * Q_BLOCK), _BF16),
            pltpu.VMEM((2, GROUP, Q_BLOCK, LANES), _F32),
            pltpu.VMEM((n_br, seq, e), _F32),
            pltpu.VMEM((n_br, seq, LANES), _F32),
        ],
        compiler_params=_params(2),
        name="dilated_attention",
    )(slopes, tables, kvq4, kvq4, kvq4)


def _out_proj_kernel(a_ref, x_ref, w_ref, o_ref):
    o_ref[...] = x_ref[...] + jnp.dot(a_ref[...].astype(_BF16), w_ref[...],
                                      preferred_element_type=_F32)


def _out_proj(att, x, w, *, tm=512):
    n, d = x.shape
    return pl.pallas_call(
        _out_proj_kernel,
        grid=(n // tm,),
        in_specs=[
            pl.BlockSpec((tm, d), lambda i: (i, 0)),
            pl.BlockSpec((tm, d), lambda i: (i, 0)),
            pl.BlockSpec((d, d), lambda i: (0, 0)),
        ],
        out_specs=pl.BlockSpec((tm, d), lambda i: (i, 0)),
        out_shape=jax.ShapeDtypeStruct((n, d), _F32),
        compiler_params=_params(1),
        name="out_proj",
    )(att, x, w)


def kernel(x, a_norm_g, conv_w1, conv_b1, conv_dw, conv_dw_b, conv_ln_g, conv_ln_b, conv_w2, conv_b2,
           kv_norm_g, w_k, w_v, b_norm_g, w_q, w_o, ffn_norm_g, ffn_w_gate, ffn_w_up, ffn_w_down,
           final_norm_g):
    batch, seq, d = x.shape
    n_a = a_norm_g.shape[0]
    n_b = b_norm_g.shape[0]
    depth = n_a + n_b
    assert d % N_HEADS == 0 and d // N_HEADS == LANES
    assert BRANCH_DILATIONS == (1, 4, 16) and seq % (max(BRANCH_DILATIONS) * Q_BLOCK * 2) == 0

    row = lambda v: v.reshape(1, -1)
    h = x.reshape(batch * seq, d)
    head_dim = d // N_HEADS
    slopes = jnp.exp2(-8.0 * jnp.arange(1, N_HEADS + 1, dtype=_F32) / N_HEADS)
    tables = jnp.asarray(_step_tables())
    kvq = None

    for layer in range(depth):
        if layer < n_a:
            a = layer
            w1 = _cast_stack([(conv_w1, a)])[0]
            w2 = _cast_stack([(conv_w2, a)])[0]
            u = _glu_in(h, row(a_norm_g[a]), w1, row(conv_b1[a]))
            h = _conv_out(u, h, conv_dw[a], row(conv_dw_b[a]), row(conv_ln_g[a]), row(conv_ln_b[a]),
                          w2, row(conv_b2[a]), seq=seq)
        else:
            i = layer - n_a
            assert n_b == 1, "one attention layer supported"
            w_kvq = _cast_stack([(w_k, None), (w_v, None), (w_q, i)])
            kvq = _kvq(h, row(kv_norm_g), row(b_norm_g[i]), w_kvq,
                       q_scale=head_dim ** -0.5 * LOG2E)
            att = _attention(kvq, slopes, tables, batch=batch, seq=seq)
            h = _out_proj(att.reshape(batch * seq, d), h, _cast_stack([(w_o, i)])[0])
        last = layer == depth - 1
        h = _ffn(h, row(ffn_norm_g[layer]),
                 _cast_stack([(ffn_w_gate, layer), (ffn_w_up, layer)]),
                 _cast_stack([(ffn_w_down, layer)]),
                 row(final_norm_g) if last else None)
    return h.reshape(batch, seq, d)
```

```python
import functools
import math

import numpy as np
import jax
import jax.numpy as jnp
from jax import lax
from jax.experimental import pallas as pl
from jax.experimental.pallas import tpu as pltpu

RMS_EPS = 1e-6
LN_EPS = 1e-5
N_HEADS = 16
CONV_WIDTH = 31
BRANCH_DILATIONS = (1, 4, 16)
DILATION_RATIO = 4
WINDOW_STEPS = 128
Q_BLOCK = 128
GROUP = 4
MASKED = 1e30
LOG2E = math.log2(math.e)
LANES = 128

_VMEM_LIMIT = 56 * 1024 * 1024
_BF16 = jnp.bfloat16
_F32 = jnp.float32


def _params(n_axes):
    return pltpu.CompilerParams(dimension_semantics=("arbitrary",) * n_axes,
                                vmem_limit_bytes=_VMEM_LIMIT)


def _sigmoid(x):
    return 1.0 / (1.0 + jnp.exp(-x))


def _rmsnorm_rows(x_ref, out_refs, g_refs, chunk=256):
    n = x_ref.shape[0] // chunk

    def body(i, carry):
        r = pl.multiple_of(i * chunk, chunk)
        x = x_ref[pl.ds(r, chunk), :]
        xhat = x * lax.rsqrt(jnp.mean(x * x, axis=-1, keepdims=True) + RMS_EPS)
        for o_ref, g_ref in zip(out_refs, g_refs):
            o_ref[pl.ds(r, chunk), :] = (xhat * g_ref[...]).astype(o_ref.dtype)
        return carry

    lax.fori_loop(0, n, body, 0)


def _cast_kernel(*refs):
    *src_refs, o_ref = refs
    which = pl.program_id(0)
    for k, src_ref in enumerate(src_refs):
        @pl.when(which == k)
        def _(src_ref=src_ref):
            o_ref[...] = src_ref[...].astype(o_ref.dtype)


def _cast_stack(sources):
    rows, cols = sources[0][0].shape[-2:]
    block_elems = (1 << 21) if len(sources) <= 2 else (1 << 20)
    rb = 1 << ((block_elems // cols).bit_length() - 1)
    while rows % rb:
        rb //= 2
    assert rb >= 16
    nb = rows // rb

    def in_spec(k, layer):
        def block(which, b):
            return jnp.where(which == k, b, jnp.where(which < k, 0, nb - 1))
        if layer is None:
            return pl.BlockSpec((rb, cols), lambda which, b: (block(which, b), 0))
        return pl.BlockSpec((None, rb, cols), lambda which, b: (layer, block(which, b), 0))

    return pl.pallas_call(
        _cast_kernel,
        grid=(len(sources), nb),
        in_specs=[in_spec(k, layer) for k, (_, layer) in enumerate(sources)],
        out_specs=pl.BlockSpec((None, rb, cols), lambda which, b: (which, b, 0)),
        out_shape=jax.ShapeDtypeStruct((len(sources), rows, cols), _BF16),
        compiler_params=_params(2),
        name="cast_bf16",
    )(*[a for a, _ in sources])


def _glu_in_kernel(x_ref, g_ref, wa_ref, wg_ref, ba_ref, bg_ref, u_ref, xn_ref):
    @pl.when(pl.program_id(1) == 0)
    def _():
        _rmsnorm_rows(x_ref, [xn_ref], [g_ref])

    xn = xn_ref[...]
    a = jnp.dot(xn, wa_ref[...], preferred_element_type=_F32) + ba_ref[...]
    gate = jnp.dot(xn, wg_ref[...], preferred_element_type=_F32) + bg_ref[...]
    u_ref[...] = a * _sigmoid(gate)


def _glu_in(x, g, w1, b1, *, tm=1024, tn=1024):
    n, d = x.shape
    nj = d // tn
    return pl.pallas_call(
        _glu_in_kernel,
        grid=(n // tm, nj),
        in_specs=[
            pl.BlockSpec((tm, d), lambda i, j: (i, 0)),
            pl.BlockSpec((1, d), lambda i, j: (0, 0)),
            pl.BlockSpec((d, tn), lambda i, j: (0, j)),
            pl.BlockSpec((d, tn), lambda i, j: (0, j + nj)),
            pl.BlockSpec((1, tn), lambda i, j: (0, j)),
            pl.BlockSpec((1, tn), lambda i, j: (0, j + nj)),
        ],
        out_specs=pl.BlockSpec((tm, tn), lambda i, j: (i, j)),
        out_shape=jax.ShapeDtypeStruct((n, d), _F32),
        scratch_shapes=[pltpu.VMEM((tm, d), _BF16)],
        compiler_params=_params(2),
        name="glu_in",
    )(x, g, w1, w1, b1, b1)


_HALO = 32
_CONV_ROWS = 256


def _conv_out_kernel(u_ref, halo_ref, x_ref, dw_ref, dwb_ref, lng_ref, lnb_ref, w2_ref, b2_ref,
                     o_ref, buf_ref, y_ref, z_ref, *, blocks_per_seq):
    tm, d = u_ref.shape
    first = (pl.program_id(0) % blocks_per_seq) == 0
    for c in range(d // LANES):
        lanes = slice(c * LANES, (c + 1) * LANES)
        buf_ref[c, 0:_HALO, :] = jnp.where(first, 0.0, halo_ref[:, lanes])
        buf_ref[c, _HALO:, :] = u_ref[:, lanes]

    def conv_lanes(c, carry):
        lanes = pl.ds(pl.multiple_of(c * LANES, LANES), LANES)
        w = dw_ref[:, lanes]
        for r0 in range(0, tm, _CONV_ROWS):
            acc = jnp.broadcast_to(dwb_ref[:, lanes], (_CONV_ROWS, LANES))
            for k in range(CONV_WIDTH):
                shifted = buf_ref[c, pl.ds(r0 + _HALO - (CONV_WIDTH - 1) + k, _CONV_ROWS, stride=1), :]
                acc = acc + shifted * w[k:k + 1, :]
            y_ref[r0:r0 + _CONV_ROWS, lanes] = acc
        return carry

    lax.fori_loop(0, d // LANES, conv_lanes, 0)

    chunk = 256

    def norm_rows(i, carry):
        rows = pl.ds(pl.multiple_of(i * chunk, chunk), chunk)
        y = y_ref[rows, :]
        mu = jnp.mean(y, axis=-1, keepdims=True)
        yc = y - mu
        var = jnp.mean(yc * yc, axis=-1, keepdims=True)
        zn = yc * lax.rsqrt(var + LN_EPS) * lng_ref[...] + lnb_ref[...]
        z_ref[rows, :] = (zn * _sigmoid(zn)).astype(z_ref.dtype)
        return carry

    lax.fori_loop(0, tm // chunk, norm_rows, 0)

    o_ref[...] = (x_ref[...] + b2_ref[...]
                  + jnp.dot(z_ref[...], w2_ref[...], preferred_element_type=_F32))


def _conv_out(u, x, dw, dwb, lng, lnb, w2, b2, *, seq, tm=512):
    n, d = u.shape
    per_halo = tm // _HALO
    kern = functools.partial(_conv_out_kernel, blocks_per_seq=seq // tm)
    row = lambda i: (0, 0)
    return pl.pallas_call(
        kern,
        grid=(n // tm,),
        in_specs=[
            pl.BlockSpec((tm, d), lambda i: (i, 0)),
            pl.BlockSpec((_HALO, d), lambda i: (jnp.maximum(i * per_halo - 1, 0), 0)),
            pl.BlockSpec((tm, d), lambda i: (i, 0)),
            pl.BlockSpec((CONV_WIDTH, d), row),
            pl.BlockSpec((1, d), row),
            pl.BlockSpec((1, d), row),
            pl.BlockSpec((1, d), row),
            pl.BlockSpec((d, d), row, pipeline_mode=pl.Buffered(1)),
            pl.BlockSpec((1, d), row),
        ],
        out_specs=pl.BlockSpec((tm, d), lambda i: (i, 0)),
        out_shape=jax.ShapeDtypeStruct((n, d), _F32),
        scratch_shapes=[pltpu.VMEM((d // LANES, tm + _HALO, LANES), _F32),
                        pltpu.VMEM((tm, d), _F32),
                        pltpu.VMEM((tm, d), _BF16)],
        compiler_params=_params(1),
        name="conv_out",
    )(u, u, x, dw, dwb, lng, lnb, w2, b2)


def _ffn_kernel(x_ref, g_ref, wgu_hbm, wd_hbm, *rest, final_norm, tf):
    if final_norm:
        gf_ref, o_ref, xn_ref, wg_buf, wu_buf, wd_buf, sem = rest
    else:
        o_ref, xn_ref, wg_buf, wu_buf, wd_buf, sem = rest
    n_chunks = wd_hbm.shape[1] // tf
    i = pl.program_id(0)
    n_tiles = pl.num_programs(0)

    def chunk_copies(j, slot):
        cols = pl.ds(pl.multiple_of(j * tf, tf), tf)
        return (pltpu.make_async_copy(wgu_hbm.at[0, :, cols], wg_buf.at[slot], sem.at[0, slot]),
                pltpu.make_async_copy(wgu_hbm.at[1, :, cols], wu_buf.at[slot], sem.at[1, slot]),
                pltpu.make_async_copy(wd_hbm.at[0, cols, :], wd_buf.at[slot], sem.at[2, slot]))

    @pl.when(i == 0)
    def _():
        for copy in chunk_copies(0, 0):
            copy.start()

    _rmsnorm_rows(x_ref, [xn_ref], [g_ref])
    o_ref[...] = x_ref[...]

    def body(j, carry):
        count = i * n_chunks + j
        slot = count % 2
        is_last = jnp.logical_and(i == n_tiles - 1, j == n_chunks - 1)

        @pl.when(jnp.logical_not(is_last))
        def _():
            for copy in chunk_copies((j + 1) % n_chunks, 1 - slot):
                copy.start()

        for copy in chunk_copies(j, slot):
            copy.wait()
        xn = xn_ref[...]
        gate = jnp.dot(xn, wg_buf[slot], preferred_element_type=_F32)
        up = jnp.dot(xn, wu_buf[slot], preferred_element_type=_F32)
        act = (gate * _sigmoid(gate) * up).astype(_BF16)
        o_ref[...] += jnp.dot(act, wd_buf[slot], preferred_element_type=_F32)
        return carry

    lax.fori_loop(0, n_chunks, body, 0)

    if final_norm:
        _rmsnorm_rows(o_ref, [o_ref], [gf_ref])


def _ffn(x, g, w_gate_up, w_down, gf=None, *, tm=1024, tf=512):
    n, d = x.shape
    f = w_gate_up.shape[2]
    assert f % tf == 0
    final_norm = gf is not None
    in_specs = [
        pl.BlockSpec((tm, d), lambda i: (i, 0)),
        pl.BlockSpec((1, d), lambda i: (0, 0)),
        pl.BlockSpec(memory_space=pl.ANY),
        pl.BlockSpec(memory_space=pl.ANY),
    ]
    args = [x, g, w_gate_up, w_down]
    if final_norm:
        in_specs.append(pl.BlockSpec((1, d), lambda i: (0, 0)))
        args.append(gf)
    return pl.pallas_call(
        functools.partial(_ffn_kernel, final_norm=final_norm, tf=tf),
        grid=(n // tm,),
        in_specs=in_specs,
        out_specs=pl.BlockSpec((tm, d), lambda i: (i, 0)),
        out_shape=jax.ShapeDtypeStruct((n, d), _F32),
        scratch_shapes=[pltpu.VMEM((tm, d), _BF16),
                        pltpu.VMEM((2, d, tf), _BF16),
                        pltpu.VMEM((2, d, tf), _BF16),
                        pltpu.VMEM((2, tf, d), _BF16),
                        pltpu.SemaphoreType.DMA((3, 2))],
        compiler_params=_params(1),
        name="ffn_final" if final_norm else "ffn",
    )(*args)


def _kvq_kernel(x_ref, gkv_ref, gq_ref, w_ref, o_ref, xkv_ref, xq_ref, *, q_scale, blocks_per_proj):
    j = pl.program_id(1)

    @pl.when(j == 0)
    def _():
        _rmsnorm_rows(x_ref, [xkv_ref], [gkv_ref])
        _rmsnorm_rows(x_ref, [xq_ref], [gq_ref])

    @pl.when(j < 2 * blocks_per_proj)
    def _():
        o_ref[...] = jnp.dot(xkv_ref[...], w_ref[...], preferred_element_type=_F32)

    @pl.when(j >= 2 * blocks_per_proj)
    def _():
        o_ref[...] = jnp.dot(xq_ref[...], w_ref[...], preferred_element_type=_F32) * q_scale


def _kvq(x, gkv, gq, w_kvq, *, q_scale, tm=1024, tn=1024):
    n, d = x.shape
    per = d // tn
    return pl.pallas_call(
        functools.partial(_kvq_kernel, q_scale=q_scale, blocks_per_proj=per),
        grid=(n // tm, 3 * per),
        in_specs=[
            pl.BlockSpec((tm, d), lambda i, j: (i, 0)),
            pl.BlockSpec((1, d), lambda i, j: (0, 0)),
            pl.BlockSpec((1, d), lambda i, j: (0, 0)),
            pl.BlockSpec((None, d, tn), lambda i, j: (j // per, 0, j % per)),
        ],
        out_specs=pl.BlockSpec((None, tm, tn), lambda i, j: (j // per, i, j % per)),
        out_shape=jax.ShapeDtypeStruct((3, n, d), _F32),
        scratch_shapes=[pltpu.VMEM((tm, d), _BF16), pltpu.VMEM((tm, d), _BF16)],
        compiler_params=_params(2),
        name="kvq",
    )(x, gkv, gq, w_kvq)


def _step_tables():
    p = np.arange(Q_BLOCK)[:, None]
    c = np.arange(2 * Q_BLOCK)[None, :]
    pair = []
    for shift in (0, Q_BLOCK):
        j = shift + p - c
        valid = (j >= 0) & (j <= WINDOW_STEPS)
        pair.append(np.where(valid, j.astype(np.float32), np.float32(MASKED)))
    return np.stack(pair).astype(np.float32)


def _attention_kernel(slopes_ref, tab_ref, q_ref, k_ref, v_ref, o_ref,
                      bias_ref, qs_ref, ks_ref, vs_ref, t4_ref, s_ref, e_ref, rl_ref, ob_ref, lse_ref):
    seq = q_ref.shape[0]
    blocks_per_branch = seq // Q_BLOCK
    n_br = len(BRANCH_DILATIONS)
    n_groups = n_br * blocks_per_branch // GROUP
    assert blocks_per_branch % GROUP == 0 and n_groups % 2 == 0 and n_groups >= 4
    h = pl.program_id(1)
    neg_slope2 = -slopes_ref[h] * LOG2E
    for b, dil in enumerate(BRANCH_DILATIONS):
        for t in range(2):
            bias_ref[b, t] = tab_ref[t] * (dil * neg_slope2)

    ratio = DILATION_RATIO
    chunk = seq // max(BRANCH_DILATIONS)
    quarter = seq // ratio

    def finer(c):
        return pl.ds((c // ratio) + ratio * (c % ratio) * chunk, chunk, stride=ratio)

    def finer16(c):
        return pl.ds((c % ratio) * quarter + c // ratio, chunk, stride=ratio)

    srcs = (q_ref, k_ref, v_ref)
    dsts = (qs_ref, ks_ref, vs_ref)

    def regroup4(c, carry):
        dst = pl.ds(pl.multiple_of(c * chunk, chunk), chunk)
        for a in range(3):
            dsts[a][0, dst, :] = srcs[a][dst, :].astype(_BF16)
            by4 = srcs[a][finer(c), :]
            t4_ref[a, dst, :] = by4
            dsts[a][1, dst, :] = by4.astype(_BF16)
        return carry

    lax.fori_loop(0, seq // chunk, regroup4, 0)

    def regroup16(c, carry):
        dst = pl.ds(pl.multiple_of(c * chunk, chunk), chunk)
        for a in range(3):
            dsts[a][2, dst, :] = t4_ref[a, finer16(c), :].astype(_BF16)
        return carry

    lax.fori_loop(0, seq // chunk, regroup16, 0)

    def locate(g, u):
        t = g * GROUP + u
        b = t // blocks_per_branch
        jb = t % blocks_per_branch
        per_class = lax.shift_right_logical(blocks_per_branch, 2 * b)
        is_first = (jb & (per_class - 1)) == 0
        q0 = pl.multiple_of(jb * Q_BLOCK, Q_BLOCK)
        k0 = pl.multiple_of(jnp.where(is_first, q0, q0 - Q_BLOCK), Q_BLOCK)
        return b, q0, k0, jnp.where(is_first, 0, 1)

    def scores(g, slot):
        for u in range(GROUP):
            b, q0, k0, _ = locate(g, u)
            q = qs_ref[b, pl.ds(q0, Q_BLOCK), :]
            k = ks_ref[b, pl.ds(k0, 2 * Q_BLOCK), :]
            s_ref[slot, u] = lax.dot_general(q, k, (((1,), (1,)), ((), ())),
                                             preferred_element_type=_F32)

    def softmax(g, slot):
        for u in range(GROUP):
            b, q0, _, tab = locate(g, u)
            s = s_ref[slot, u] + bias_ref[b, tab]
            m = jnp.max(s, axis=-1, keepdims=True)
            e = jnp.exp2(s - m)
            l = jnp.sum(e, axis=-1, keepdims=True)
            e_ref[slot, u] = e.astype(_BF16)
            rl_ref[slot, u] = jnp.broadcast_to(1.0 / l, (Q_BLOCK, LANES))
            lse_ref[b, pl.ds(q0, Q_BLOCK), :] = jnp.broadcast_to(m + jnp.log2(l), (Q_BLOCK, LANES))

    def values(g, slot):
        for u in range(GROUP):
            b, q0, k0, _ = locate(g, u)
            v = vs_ref[b, pl.ds(k0, 2 * Q_BLOCK), :]
            acc = jnp.dot(e_ref[slot, u], v, preferred_element_type=_F32)
            ob_ref[b, pl.ds(q0, Q_BLOCK), :] = acc * rl_ref[slot, u]

    zero = jnp.int32(0)
    scores(zero, 0)
    softmax(zero, 0)
    scores(zero + 1, 1)

    def trip(i, carry):
        g = 2 * i
        values(g - 2, 0)
        softmax(g - 1, 1)
        scores(g, 0)
        values(g - 1, 1)
        softmax(g, 0)
        scores(g + 1, 1)
        return carry

    lax.fori_loop(1, n_groups // 2, trip, 0)
    values(zero + (n_groups - 2), 0)
    softmax(zero + (n_groups - 1), 1)
    values(zero + (n_groups - 1), 1)

    def coarsen(c, carry):
        src = pl.ds(pl.multiple_of(c * chunk, chunk), chunk)
        t4_ref[0, finer16(c), :] = ob_ref[2, src, :]
        t4_ref[1, finer16(c), :] = lse_ref[2, src, :]
        return carry

    lax.fori_loop(0, seq // chunk, coarsen, 0)

    rows_per = 256

    def merge(c, carry):
        by4 = pl.ds(pl.multiple_of(c * rows_per, rows_per), rows_per)
        per_class = quarter // rows_per
        natural = pl.ds((c // per_class) + ratio * (c % per_class) * rows_per, rows_per, stride=ratio)
        ls = [lse_ref[0, natural, :], lse_ref[1, by4, :], t4_ref[1, by4, :]]
        os = [ob_ref[0, natural, :], ob_ref[1, by4, :], t4_ref[0, by4, :]]
        l_max = functools.reduce(jnp.maximum, ls)
        num = 0.0
        den = 0.0
        for b in range(n_br):
            w = jnp.exp2(ls[b] - l_max)
            num = num + w * os[b]
            den = den + w
        o_ref[natural, :] = num / den
        return carry

    lax.fori_loop(0, seq // rows_per, merge, 0)


def _attention(kvq, slopes, tables, *, batch, seq):
    _, n, d = kvq.shape
    e = d // N_HEADS
    n_br = len(BRANCH_DILATIONS)
    kvq4 = kvq.reshape(3, batch, seq, d)

    def qkv_spec(which):
        return pl.BlockSpec((None, None, seq, e), lambda b, h: (which, b, 0, h))

    return pl.pallas_call(
        _attention_kernel,
        grid=(batch, N_HEADS),
        in_specs=[
            pl.BlockSpec(memory_space=pltpu.SMEM),
            pl.BlockSpec(tables.shape, lambda b, h: (0, 0, 0)),
            qkv_spec(2), qkv_spec(0), qkv_spec(1),
        ],
        out_specs=pl.BlockSpec((None, seq, e), lambda b, h: (b, 0, h)),
        out_shape=jax.ShapeDtypeStruct((batch, seq, d), _F32),
        scratch_shapes=[
            pltpu.VMEM((n_br, 2, Q_BLOCK, 2 * Q_BLOCK), _F32),
            pltpu.VMEM((n_br, seq, e), _BF16),
            pltpu.VMEM((n_br, seq, e), _BF16),
            pltpu.VMEM((n_br, seq, e), _BF16),
            pltpu.VMEM((3, seq, e), _F32),
            pltpu.VMEM((2, GROUP, Q_BLOCK, 2 * Q_BLOCK), _F32),
            pltpu.VMEM((2, GROUP, Q_BLOCK, 2 * Q_BLOCK), _BF16),
            pltpu.VMEM((2, GROUP, Q_BLOCK, LANES), _F32),
            pltpu.VMEM((n_br, seq, e), _F32),
            pltpu.VMEM((n_br, seq, LANES), _F32),
        ],
        compiler_params=_params(2),
        name="dilated_attention",
    )(slopes, tables, kvq4, kvq4, kvq4)


def _out_proj_kernel(a_ref, x_ref, w_ref, o_ref):
    o_ref[...] = x_ref[...] + jnp.dot(a_ref[...].astype(_BF16), w_ref[...],
                                      preferred_element_type=_F32)


def _out_proj(att, x, w, *, tm=512):
    n, d = x.shape
    return pl.pallas_call(
        _out_proj_kernel,
        grid=(n // tm,),
        in_specs=[
            pl.BlockSpec((tm, d), lambda i: (i, 0)),
            pl.BlockSpec((tm, d), lambda i: (i, 0)),
            pl.BlockSpec((d, d), lambda i: (0, 0)),
        ],
        out_specs=pl.BlockSpec((tm, d), lambda i: (i, 0)),
        out_shape=jax.ShapeDtypeStruct((n, d), _F32),
        compiler_params=_params(1),
        name="out_proj",
    )(att, x, w)


def kernel(x, a_norm_g, conv_w1, conv_b1, conv_dw, conv_dw_b, conv_ln_g, conv_ln_b, conv_w2, conv_b2,
           kv_norm_g, w_k, w_v, b_norm_g, w_q, w_o, ffn_norm_g, ffn_w_gate, ffn_w_up, ffn_w_down,
           final_norm_g):
    batch, seq, d = x.shape
    n_a = a_norm_g.shape[0]
    n_b = b_norm_g.shape[0]
    depth = n_a + n_b
    assert d % N_HEADS == 0 and d // N_HEADS == LANES
    assert BRANCH_DILATIONS == (1, 4, 16) and seq % (max(BRANCH_DILATIONS) * Q_BLOCK * 2) == 0

    row = lambda v: v.reshape(1, -1)
    h = x.reshape(batch * seq, d)
    head_dim = d // N_HEADS
    slopes = jnp.exp2(-8.0 * jnp.arange(1, N_HEADS + 1, dtype=_F32) / N_HEADS)
    tables = jnp.asarray(_step_tables())
    kvq = None

    for layer in range(depth):
        if layer < n_a:
            a = layer
            w1 = _cast_stack([(conv_w1, a)])[0]
            w2 = _cast_stack([(conv_w2, a)])[0]
            u = _glu_in(h, row(a_norm_g[a]), w1, row(conv_b1[a]))
            h = _conv_out(u, h, conv_dw[a], row(conv_dw_b[a]), row(conv_ln_g[a]), row(conv_ln_b[a]),
                          w2, row(conv_b2[a]), seq=seq)
        else:
            i = layer - n_a
            assert n_b == 1, "one attention layer supported"
            w_kvq = _cast_stack([(w_k, None), (w_v, None), (w_q, i)])
            kvq = _kvq(h, row(kv_norm_g), row(b_norm_g[i]), w_kvq,
                       q_scale=head_dim ** -0.5 * LOG2E)
            att = _attention(kvq, slopes, tables, batch=batch, seq=seq)
            h = _out_proj(att.reshape(batch * seq, d), h, _cast_stack([(w_o, i)])[0])
        last = layer == depth - 1
        h = _ffn(h, row(ffn_norm_g[layer]),
                 _cast_stack([(ffn_w_gate, layer), (ffn_w_up, layer)]),
                 _cast_stack([(ffn_w_down, layer)]),
                 row(final_norm_g) if last else None)
    return h.reshape(batch, seq, d)
```

```python
import functools
import math

import numpy as np
import jax
import jax.numpy as jnp
from jax import lax
from jax.experimental import pallas as pl
from jax.experimental.pallas import tpu as pltpu

RMS_EPS = 1e-6
LN_EPS = 1e-5
N_HEADS = 16
CONV_WIDTH = 31
BRANCH_DILATIONS = (1, 4, 16)
DILATION_RATIO = 4
WINDOW_STEPS = 128
Q_BLOCK = 128
GROUP = 4
MASKED = 1e30
LOG2E = math.log2(math.e)
LANES = 128

_VMEM_LIMIT = 56 * 1024 * 1024
_BF16 = jnp.bfloat16
_F32 = jnp.float32


def _params(n_axes):
    return pltpu.CompilerParams(dimension_semantics=("arbitrary",) * n_axes,
                                vmem_limit_bytes=_VMEM_LIMIT)


def _sigmoid(x):
    return 1.0 / (1.0 + jnp.exp(-x))


def _rmsnorm_rows(x_ref, out_refs, g_refs, chunk=256):
    n = x_ref.shape[0] // chunk

    def body(i, carry):
        r = pl.multiple_of(i * chunk, chunk)
        x = x_ref[pl.ds(r, chunk), :]
        xhat = x * lax.rsqrt(jnp.mean(x * x, axis=-1, keepdims=True) + RMS_EPS)
        for o_ref, g_ref in zip(out_refs, g_refs):
            o_ref[pl.ds(r, chunk), :] = (xhat * g_ref[...]).astype(o_ref.dtype)
        return carry

    lax.fori_loop(0, n, body, 0)


def _cast_kernel(*refs):
    *src_refs, o_ref = refs
    which = pl.program_id(0)
    for k, src_ref in enumerate(src_refs):
        @pl.when(which == k)
        def _(src_ref=src_ref):
            o_ref[...] = src_ref[...].astype(o_ref.dtype)


def _cast_stack(sources):
    rows, cols = sources[0][0].shape[-2:]
    block_elems = (1 << 21) if len(sources) <= 2 else (1 << 20)
    rb = 1 << ((block_elems // cols).bit_length() - 1)
    while rows % rb:
        rb //= 2
    assert rb >= 16
    nb = rows // rb

    def in_spec(k, layer):
        def block(which, b):
            return jnp.where(which == k, b, jnp.where(which < k, 0, nb - 1))
        if layer is None:
            return pl.BlockSpec((rb, cols), lambda which, b: (block(which, b), 0))
        return pl.BlockSpec((None, rb, cols), lambda which, b: (layer, block(which, b), 0))

    return pl.pallas_call(
        _cast_kernel,
        grid=(len(sources), nb),
        in_specs=[in_spec(k, layer) for k, (_, layer) in enumerate(sources)],
        out_specs=pl.BlockSpec((None, rb, cols), lambda which, b: (which, b, 0)),
        out_shape=jax.ShapeDtypeStruct((len(sources), rows, cols), _BF16),
        compiler_params=_params(2),
        name="cast_bf16",
    )(*[a for a, _ in sources])


def _glu_in_kernel(x_ref, g_ref, wa_ref, wg_ref, ba_ref, bg_ref, u_ref, xn_ref):
    @pl.when(pl.program_id(1) == 0)
    def _():
        _rmsnorm_rows(x_ref, [xn_ref], [g_ref])

    xn = xn_ref[...]
    a = jnp.dot(xn, wa_ref[...], preferred_element_type=_F32) + ba_ref[...]
    gate = jnp.dot(xn, wg_ref[...], preferred_element_type=_F32) + bg_ref[...]
    u_ref[...] = a * _sigmoid(gate)


def _glu_in(x, g, w1, b1, *, tm=1024, tn=1024):
    n, d = x.shape
    nj = d // tn
    return pl.pallas_call(
        _glu_in_kernel,
        grid=(n // tm, nj),
        in_specs=[
            pl.BlockSpec((tm, d), lambda i, j: (i, 0)),
            pl.BlockSpec((1, d), lambda i, j: (0, 0)),
            pl.BlockSpec((d, tn), lambda i, j: (0, j)),
            pl.BlockSpec((d, tn), lambda i, j: (0, j + nj)),
            pl.BlockSpec((1, tn), lambda i, j: (0, j)),
            pl.BlockSpec((1, tn), lambda i, j: (0, j + nj)),
        ],
        out_specs=pl.BlockSpec((tm, tn), lambda i, j: (i, j)),
        out_shape=jax.ShapeDtypeStruct((n, d), _F32),
        scratch_shapes=[pltpu.VMEM((tm, d), _BF16)],
        compiler_params=_params(2),
        name="glu_in",
    )(x, g, w1, w1, b1, b1)


_HALO = 32
_CONV_ROWS = 256


def _conv_out_kernel(u_ref, halo_ref, x_ref, dw_ref, dwb_ref, lng_ref, lnb_ref, w2_ref, b2_ref,
                     o_ref, buf_ref, y_ref, z_ref, *, blocks_per_seq):
    tm, d = u_ref.shape
    first = (pl.program_id(0) % blocks_per_seq) == 0
    for c in range(d // LANES):
        lanes = slice(c * LANES, (c + 1) * LANES)
        buf_ref[c, 0:_HALO, :] = jnp.where(first, 0.0, halo_ref[:, lanes])
        buf_ref[c, _HALO:, :] = u_ref[:, lanes]

    def conv_lanes(c, carry):
        lanes = pl.ds(pl.multiple_of(c * LANES, LANES), LANES)
        w = dw_ref[:, lanes]
        for r0 in range(0, tm, _CONV_ROWS):
            acc = jnp.broadcast_to(dwb_ref[:, lanes], (_CONV_ROWS, LANES))
            for k in range(CONV_WIDTH):
                shifted = buf_ref[c, pl.ds(r0 + _HALO - (CONV_WIDTH - 1) + k, _CONV_ROWS, stride=1), :]
                acc = acc + shifted * w[k:k + 1, :]
            y_ref[r0:r0 + _CONV_ROWS, lanes] = acc
        return carry

    lax.fori_loop(0, d // LANES, conv_lanes, 0)

    chunk = 256

    def norm_rows(i, carry):
        rows = pl.ds(pl.multiple_of(i * chunk, chunk), chunk)
        y = y_ref[rows, :]
        mu = jnp.mean(y, axis=-1, keepdims=True)
        yc = y - mu
        var = jnp.mean(yc * yc, axis=-1, keepdims=True)
        zn = yc * lax.rsqrt(var + LN_EPS) * lng_ref[...] + lnb_ref[...]
        z_ref[rows, :] = (zn * _sigmoid(zn)).astype(z_ref.dtype)
        return carry

    lax.fori_loop(0, tm // chunk, norm_rows, 0)

    o_ref[...] = (x_ref[...] + b2_ref[...]
                  + jnp.dot(z_ref[...], w2_ref[...], preferred_element_type=_F32))


def _conv_out(u, x, dw, dwb, lng, lnb, w2, b2, *, seq, tm=512):
    n, d = u.shape
    per_halo = tm // _HALO
    kern = functools.partial(_conv_out_kernel, blocks_per_seq=seq // tm)
    row = lambda i: (0, 0)
    return pl.pallas_call(
        kern,
        grid=(n // tm,),
        in_specs=[
            pl.BlockSpec((tm, d), lambda i: (i, 0)),
            pl.BlockSpec((_HALO, d), lambda i: (jnp.maximum(i * per_halo - 1, 0), 0)),
            pl.BlockSpec((tm, d), lambda i: (i, 0)),
            pl.BlockSpec((CONV_WIDTH, d), row),
            pl.BlockSpec((1, d), row),
            pl.BlockSpec((1, d), row),
            pl.BlockSpec((1, d), row),
            pl.BlockSpec((d, d), row, pipeline_mode=pl.Buffered(1)),
            pl.BlockSpec((1, d), row),
        ],
        out_specs=pl.BlockSpec((tm, d), lambda i: (i, 0)),
        out_shape=jax.ShapeDtypeStruct((n, d), _F32),
        scratch_shapes=[pltpu.VMEM((d // LANES, tm + _HALO, LANES), _F32),
                        pltpu.VMEM((tm, d), _F32),
                        pltpu.VMEM((tm, d), _BF16)],
        compiler_params=_params(1),
        name="conv_out",
    )(u, u, x, dw, dwb, lng, lnb, w2, b2)


def _ffn_kernel(x_ref, g_ref, wgu_hbm, wd_hbm, *rest, final_norm, tf):
    if final_norm:
        gf_ref, o_ref, xn_ref, wg_buf, wu_buf, wd_buf, sem = rest
    else:
        o_ref, xn_ref, wg_buf, wu_buf, wd_buf, sem = rest
    n_chunks = wd_hbm.shape[1] // tf
    i = pl.program_id(0)
    n_tiles = pl.num_programs(0)

    def chunk_copies(j, slot):
        cols = pl.ds(pl.multiple_of(j * tf, tf), tf)
        return (pltpu.make_async_copy(wgu_hbm.at[0, :, cols], wg_buf.at[slot], sem.at[0, slot]),
                pltpu.make_async_copy(wgu_hbm.at[1, :, cols], wu_buf.at[slot], sem.at[1, slot]),
                pltpu.make_async_copy(wd_hbm.at[0, cols, :], wd_buf.at[slot], sem.at[2, slot]))

    @pl.when(i == 0)
    def _():
        for copy in chunk_copies(0, 0):
            copy.start()

    _rmsnorm_rows(x_ref, [xn_ref], [g_ref])
    o_ref[...] = x_ref[...]

    def body(j, carry):
        count = i * n_chunks + j
        slot = count % 2
        is_last = jnp.logical_and(i == n_tiles - 1, j == n_chunks - 1)

        @pl.when(jnp.logical_not(is_last))
        def _():
            for copy in chunk_copies((j + 1) % n_chunks, 1 - slot):
                copy.start()

        for copy in chunk_copies(j, slot):
            copy.wait()
        xn = xn_ref[...]
        gate = jnp.dot(xn, wg_buf[slot], preferred_element_type=_F32)
        up = jnp.dot(xn, wu_buf[slot], preferred_element_type=_F32)
        act = (gate * _sigmoid(gate) * up).astype(_BF16)
        o_ref[...] += jnp.dot(act, wd_buf[slot], preferred_element_type=_F32)
        return carry

    lax.fori_loop(0, n_chunks, body, 0)

    if final_norm:
        _rmsnorm_rows(o_ref, [o_ref], [gf_ref])


def _ffn(x, g, w_gate_up, w_down, gf=None, *, tm=1024, tf=512):
    n, d = x.shape
    f = w_gate_up.shape[2]
    assert f % tf == 0
    final_norm = gf is not None
    in_specs = [
        pl.BlockSpec((tm, d), lambda i: (i, 0)),
        pl.BlockSpec((1, d), lambda i: (0, 0)),
        pl.BlockSpec(memory_space=pl.ANY),
        pl.BlockSpec(memory_space=pl.ANY),
    ]
    args = [x, g, w_gate_up, w_down]
    if final_norm:
        in_specs.append(pl.BlockSpec((1, d), lambda i: (0, 0)))
        args.append(gf)
    return pl.pallas_call(
        functools.partial(_ffn_kernel, final_norm=final_norm, tf=tf),
        grid=(n // tm,),
        in_specs=in_specs,
        out_specs=pl.BlockSpec((tm, d), lambda i: (i, 0)),
        out_shape=jax.ShapeDtypeStruct((n, d), _F32),
        scratch_shapes=[pltpu.VMEM((tm, d), _BF16),
                        pltpu.VMEM((2, d, tf), _BF16),
                        pltpu.VMEM((2, d, tf), _BF16),
                        pltpu.VMEM((2, tf, d), _BF16),
                        pltpu.SemaphoreType.DMA((3, 2))],
        compiler_params=_params(1),
        name="ffn_final" if final_norm else "ffn",
    )(*args)


def _kvq_kernel(x_ref, gkv_ref, gq_ref, w_ref, gate_ref, up_ref, down_ref, o_ref, wgu_ref, wdn_ref,
                xkv_ref, xq_ref, *, q_scale, blocks_per_proj, third):
    j = pl.program_id(1)
    step = pl.program_id(0) * pl.num_programs(1) + j

    @pl.when(j == 0)
    def _():
        _rmsnorm_rows(x_ref, [xkv_ref], [gkv_ref])
        _rmsnorm_rows(x_ref, [xq_ref], [gq_ref])

    @pl.when(j < 2 * blocks_per_proj)
    def _():
        o_ref[...] = jnp.dot(xkv_ref[...], w_ref[...], preferred_element_type=_F32)

    @pl.when(j >= 2 * blocks_per_proj)
    def _():
        o_ref[...] = jnp.dot(xq_ref[...], w_ref[...], preferred_element_type=_F32) * q_scale

    @pl.when(step < third)
    def _():
        wgu_ref[...] = gate_ref[...].astype(wgu_ref.dtype)

    @pl.when(jnp.logical_and(step >= third, step < 2 * third))
    def _():
        wgu_ref[...] = up_ref[...].astype(wgu_ref.dtype)

    @pl.when(step >= 2 * third)
    def _():
        wdn_ref[...] = down_ref[...].astype(wdn_ref.dtype)


def _kvq(x, gkv, gq, w_kvq, ffn_gate, ffn_up, ffn_down, layer, *, q_scale, tm=1024, tn=1024):
    n, d = x.shape
    f = ffn_gate.shape[2]
    per = d // tn
    steps_j = 3 * per
    third = (n // tm) * steps_j // 3
    gate_rows, down_rows = d // third, f // third
    assert third * 3 == (n // tm) * steps_j and gate_rows * third == d and down_rows * third == f
    assert gate_rows % 16 == 0 and down_rows % 16 == 0

    def block(i, j, k):
        return jnp.clip(i * steps_j + j - k * third, 0, third - 1)

    def gu_index(i, j):
        step = i * steps_j + j
        return jnp.minimum(step // third, 1), jnp.where(step < 2 * third, step % third, third - 1), 0

    return pl.pallas_call(
        functools.partial(_kvq_kernel, q_scale=q_scale, blocks_per_proj=per, third=third),
        grid=(n // tm, steps_j),
        in_specs=[
            pl.BlockSpec((tm, d), lambda i, j: (i, 0)),
            pl.BlockSpec((1, d), lambda i, j: (0, 0)),
            pl.BlockSpec((1, d), lambda i, j: (0, 0)),
            pl.BlockSpec((None, d, tn), lambda i, j: (j // per, 0, j % per)),
            pl.BlockSpec((None, gate_rows, f), lambda i, j: (layer, block(i, j, 0), 0)),
            pl.BlockSpec((None, gate_rows, f), lambda i, j: (layer, block(i, j, 1), 0)),
            pl.BlockSpec((None, down_rows, d), lambda i, j: (layer, block(i, j, 2), 0)),
        ],
        out_specs=[
            pl.BlockSpec((None, tm, tn), lambda i, j: (j // per, i, j % per)),
            pl.BlockSpec((None, gate_rows, f), gu_index),
            pl.BlockSpec((None, down_rows, d), lambda i, j: (0, block(i, j, 2), 0)),
        ],
        out_shape=[jax.ShapeDtypeStruct((3, n, d), _F32),
                   jax.ShapeDtypeStruct((2, d, f), _BF16),
                   jax.ShapeDtypeStruct((1, f, d), _BF16)],
        scratch_shapes=[pltpu.VMEM((tm, d), _BF16), pltpu.VMEM((tm, d), _BF16)],
        compiler_params=_params(2),
        name="kvq",
    )(x, gkv, gq, w_kvq, ffn_gate, ffn_up, ffn_down)


def _step_tables():
    p = np.arange(Q_BLOCK)[:, None]
    c = np.arange(2 * Q_BLOCK)[None, :]
    pair = []
    for shift in (0, Q_BLOCK):
        j = shift + p - c
        valid = (j >= 0) & (j <= WINDOW_STEPS)
        pair.append(np.where(valid, j.astype(np.float32), np.float32(MASKED)))
    return np.stack(pair).astype(np.float32)


def _attention_kernel(slopes_ref, tab_ref, q_ref, k_ref, v_ref, o_ref,
                      bias_ref, qs_ref, ks_ref, vs_ref, t4_ref, s_ref, e_ref, rl_ref, ob_ref, lse_ref):
    seq = q_ref.shape[0]
    blocks_per_branch = seq // Q_BLOCK
    n_br = len(BRANCH_DILATIONS)
    n_groups = n_br * blocks_per_branch // GROUP
    assert blocks_per_branch % GROUP == 0 and n_groups % 2 == 0 and n_groups >= 4
    h = pl.program_id(1)
    neg_slope2 = -slopes_ref[h] * LOG2E
    for b, dil in enumerate(BRANCH_DILATIONS):
        for t in range(2):
            bias_ref[b, t] = tab_ref[t] * (dil * neg_slope2)

    ratio = DILATION_RATIO
    chunk = seq // max(BRANCH_DILATIONS)
    quarter = seq // ratio

    def finer(c):
        return pl.ds((c // ratio) + ratio * (c % ratio) * chunk, chunk, stride=ratio)

    def finer16(c):
        return pl.ds((c % ratio) * quarter + c // ratio, chunk, stride=ratio)

    srcs = (q_ref, k_ref, v_ref)
    dsts = (qs_ref, ks_ref, vs_ref)

    def regroup4(c, carry):
        dst = pl.ds(pl.multiple_of(c * chunk, chunk), chunk)
        for a in range(3):
            dsts[a][0, dst, :] = srcs[a][dst, :].astype(_BF16)
            by4 = srcs[a][finer(c), :]
            t4_ref[a, dst, :] = by4
            dsts[a][1, dst, :] = by4.astype(_BF16)
        return carry

    lax.fori_loop(0, seq // chunk, regroup4, 0)

    def regroup16(c, carry):
        dst = pl.ds(pl.multiple_of(c * chunk, chunk), chunk)
        for a in range(3):
            dsts[a][2, dst, :] = t4_ref[a, finer16(c), :].astype(_BF16)
        return carry

    lax.fori_loop(0, seq // chunk, regroup16, 0)

    def locate(g, u):
        t = g * GROUP + u
        b = t // blocks_per_branch
        jb = t % blocks_per_branch
        per_class = lax.shift_right_logical(blocks_per_branch, 2 * b)
        is_first = (jb & (per_class - 1)) == 0
        q0 = pl.multiple_of(jb * Q_BLOCK, Q_BLOCK)
        k0 = pl.multiple_of(jnp.where(is_first, q0, q0 - Q_BLOCK), Q_BLOCK)
        return b, q0, k0, jnp.where(is_first, 0, 1)

    def scores(g, slot):
        for u in range(GROUP):
            b, q0, k0, _ = locate(g, u)
            q = qs_ref[b, pl.ds(q0, Q_BLOCK), :]
            k = ks_ref[b, pl.ds(k0, 2 * Q_BLOCK), :]
            s_ref[slot, u] = lax.dot_general(q, k, (((1,), (1,)), ((), ())),
                                             preferred_element_type=_F32)

    def softmax(g, slot):
        for u in range(GROUP):
            b, q0, _, tab = locate(g, u)
            s = s_ref[slot, u] + bias_ref[b, tab]
            m = jnp.max(s, axis=-1, keepdims=True)
            e = jnp.exp2(s - m)
            l = jnp.sum(e, axis=-1, keepdims=True)
            e_ref[slot, u] = e.astype(_BF16)
            rl_ref[slot, u] = jnp.broadcast_to(1.0 / l, (Q_BLOCK, LANES))
            lse_ref[b, pl.ds(q0, Q_BLOCK), :] = jnp.broadcast_to(m + jnp.log2(l), (Q_BLOCK, LANES))

    def values(g, slot):
        for u in range(GROUP):
            b, q0, k0, _ = locate(g, u)
            v = vs_ref[b, pl.ds(k0, 2 * Q_BLOCK), :]
            acc = jnp.dot(e_ref[slot, u], v, preferred_element_type=_F32)
            ob_ref[b, pl.ds(q0, Q_BLOCK), :] = acc * rl_ref[slot, u]

    zero = jnp.int32(0)
    scores(zero, 0)
    softmax(zero, 0)
    scores(zero + 1, 1)

    def trip(i, carry):
        g = 2 * i
        values(g - 2, 0)
        softmax(g - 1, 1)
        scores(g, 0)
        values(g - 1, 1)
        softmax(g, 0)
        scores(g + 1, 1)
        return carry

    lax.fori_loop(1, n_groups // 2, trip, 0)
    values(zero + (n_groups - 2), 0)
    softmax(zero + (n_groups - 1), 1)
    values(zero + (n_groups - 1), 1)

    def coarsen(c, carry):
        src = pl.ds(pl.multiple_of(c * chunk, chunk), chunk)
        t4_ref[0, finer16(c), :] = ob_ref[2, src, :]
        t4_ref[1, finer16(c), :] = lse_ref[2, src, :]
        return carry

    lax.fori_loop(0, seq // chunk, coarsen, 0)

    rows_per = 256

    def merge(c, carry):
        by4 = pl.ds(pl.multiple_of(c * rows_per, rows_per), rows_per)
        per_class = quarter // rows_per
        natural = pl.ds((c // per_class) + ratio * (c % per_class) * rows_per, rows_per, stride=ratio)
        ls = [lse_ref[0, natural, :], lse_ref[1, by4, :], t4_ref[1, by4, :]]
        os = [ob_ref[0, natural, :], ob_ref[1, by4, :], t4_ref[0, by4, :]]
        l_max = functools.reduce(jnp.maximum, ls)
        num = 0.0
        den = 0.0
        for b in range(n_br):
            w = jnp.exp2(ls[b] - l_max)
            num = num + w * os[b]
            den = den + w
        o_ref[natural, :] = num / den
        return carry

    lax.fori_loop(0, seq // rows_per, merge, 0)


def _attention(kvq, slopes, tables, *, batch, seq):
    _, n, d = kvq.shape
    e = d // N_HEADS
    n_br = len(BRANCH_DILATIONS)
    kvq4 = kvq.reshape(3, batch, seq, d)

    def qkv_spec(which):
        return pl.BlockSpec((None, None, seq, e), lambda b, h: (which, b, 0, h))

    return pl.pallas_call(
        _attention_kernel,
        grid=(batch, N_HEADS),
        in_specs=[
            pl.BlockSpec(memory_space=pltpu.SMEM),
            pl.BlockSpec(tables.shape, lambda b, h: (0, 0, 0)),
            qkv_spec(2), qkv_spec(0), qkv_spec(1),
        ],
        out_specs=pl.BlockSpec((None, seq, e), lambda b, h: (b, 0, h)),
        out_shape=jax.ShapeDtypeStruct((batch, seq, d), _F32),
        scratch_shapes=[
            pltpu.VMEM((n_br, 2, Q_BLOCK, 2 * Q_BLOCK), _F32),
            pltpu.VMEM((n_br, seq, e), _BF16),
            pltpu.VMEM((n_br, seq, e), _BF16),
            pltpu.VMEM((n_br, seq, e), _BF16),
            pltpu.VMEM((3, seq, e), _F32),
            pltpu.VMEM((2, GROUP, Q_BLOCK, 2 * Q_BLOCK), _F32),
            pltpu.VMEM((2, GROUP, Q_BLOCK, 2 * Q_BLOCK), _BF16),
            pltpu.VMEM((2, GROUP, Q_BLOCK, LANES), _F32),
            pltpu.VMEM((n_br, seq, e), _F32),
            pltpu.VMEM((n_br, seq, LANES), _F32),
        ],
        compiler_params=_params(2),
        name="dilated_attention",
    )(slopes, tables, kvq4, kvq4, kvq4)


def _out_proj_kernel(a_ref, x_ref, w_ref, o_ref):
    o_ref[...] = x_ref[...] + jnp.dot(a_ref[...].astype(_BF16), w_ref[...],
                                      preferred_element_type=_F32)


def _out_proj(att, x, w, *, tm=512):
    n, d = x.shape
    return pl.pallas_call(
        _out_proj_kernel,
        grid=(n // tm,),
        in_specs=[
            pl.BlockSpec((tm, d), lambda i: (i, 0)),
            pl.BlockSpec((tm, d), lambda i: (i, 0)),
            pl.BlockSpec((d, d), lambda i: (0, 0)),
        ],
        out_specs=pl.BlockSpec((tm, d), lambda i: (i, 0)),
        out_shape=jax.ShapeDtypeStruct((n, d), _F32),
        compiler_params=_params(1),
        name="out_proj",
    )(att, x, w)


def kernel(x, a_norm_g, conv_w1, conv_b1, conv_dw, conv_dw_b, conv_ln_g, conv_ln_b, conv_w2, conv_b2,
           kv_norm_g, w_k, w_v, b_norm_g, w_q, w_o, ffn_norm_g, ffn_w_gate, ffn_w_up, ffn_w_down,
           final_norm_g):
    batch, seq, d = x.shape
    n_a = a_norm_g.shape[0]
    n_b = b_norm_g.shape[0]
    depth = n_a + n_b
    assert d % N_HEADS == 0 and d // N_HEADS == LANES
    assert BRANCH_DILATIONS == (1, 4, 16) and seq % (max(BRANCH_DILATIONS) * Q_BLOCK * 2) == 0

    row = lambda v: v.reshape(1, -1)
    h = x.reshape(batch * seq, d)
    head_dim = d // N_HEADS
    slopes = jnp.exp2(-8.0 * jnp.arange(1, N_HEADS + 1, dtype=_F32) / N_HEADS)
    tables = jnp.asarray(_step_tables())
    kvq = None

    for layer in range(depth):
        ffn_weights = None
        if layer < n_a:
            a = layer
            w1 = _cast_stack([(conv_w1, a)])[0]
            w2 = _cast_stack([(conv_w2, a)])[0]
            u = _glu_in(h, row(a_norm_g[a]), w1, row(conv_b1[a]))
            h = _conv_out(u, h, conv_dw[a], row(conv_dw_b[a]), row(conv_ln_g[a]), row(conv_ln_b[a]),
                          w2, row(conv_b2[a]), seq=seq)
        else:
            i = layer - n_a
            assert n_b == 1, "one attention layer supported"
            w_kvq = _cast_stack([(w_k, None), (w_v, None), (w_q, i)])
            kvq, *ffn_weights = _kvq(h, row(kv_norm_g), row(b_norm_g[i]), w_kvq,
                                     ffn_w_gate, ffn_w_up, ffn_w_down, layer,
                                     q_scale=head_dim ** -0.5 * LOG2E)
            att = _attention(kvq, slopes, tables, batch=batch, seq=seq)
            h = _out_proj(att.reshape(batch * seq, d), h, _cast_stack([(w_o, i)])[0])
        last = layer == depth - 1
        if ffn_weights is None:
            ffn_weights = (_cast_stack([(ffn_w_gate, layer), (ffn_w_up, layer)]),
                           _cast_stack([(ffn_w_down, layer)]))
        h = _ffn(h, row(ffn_norm_g[layer]), *ffn_weights, row(final_norm_g) if last else None)
    return h.reshape(batch, seq, d)
```

```python
import functools
import math

import numpy as np
import jax
import jax.numpy as jnp
from jax import lax
from jax.experimental import pallas as pl
from jax.experimental.pallas import tpu as pltpu

RMS_EPS = 1e-6
LN_EPS = 1e-5
N_HEADS = 16
CONV_WIDTH = 31
BRANCH_DILATIONS = (1, 4, 16)
DILATION_RATIO = 4
WINDOW_STEPS = 128
Q_BLOCK = 128
GROUP = 4
MASKED = 1e30
LOG2E = math.log2(math.e)
LANES = 128

_VMEM_LIMIT = 56 * 1024 * 1024
_BF16 = jnp.bfloat16
_F32 = jnp.float32


def _params(n_axes):
    return pltpu.CompilerParams(dimension_semantics=("arbitrary",) * n_axes,
                                vmem_limit_bytes=_VMEM_LIMIT)


def _sigmoid(x):
    return 1.0 / (1.0 + jnp.exp(-x))


def _rmsnorm_rows(x_ref, out_refs, g_refs, chunk=256):
    n = x_ref.shape[0] // chunk

    def body(i, carry):
        r = pl.multiple_of(i * chunk, chunk)
        x = x_ref[pl.ds(r, chunk), :]
        xhat = x * lax.rsqrt(jnp.mean(x * x, axis=-1, keepdims=True) + RMS_EPS)
        for o_ref, g_ref in zip(out_refs, g_refs):
            y = xhat if g_ref is None else xhat * g_ref[...]
            o_ref[pl.ds(r, chunk), :] = y.astype(o_ref.dtype)
        return carry

    lax.fori_loop(0, n, body, 0)


def _cast_kernel(*refs, n_src, scaled):
    src_refs, scale_refs, o_ref = refs[:n_src], refs[n_src:-1], refs[-1]
    which = pl.program_id(0)
    for k, src_ref in enumerate(src_refs):
        @pl.when(which == k)
        def _(k=k, src_ref=src_ref):
            w = src_ref[...]
            if scaled:
                w = w * scale_refs[k][...]
            o_ref[...] = w.astype(o_ref.dtype)


def _cast_stack(sources, row_scales=None):
    rows, cols = sources[0][0].shape[-2:]
    block_elems = (1 << 21) if len(sources) <= 2 else (1 << 20)
    rb = 1 << ((block_elems // cols).bit_length() - 1)
    while rows % rb:
        rb //= 2
    assert rb >= 16
    nb = rows // rb

    def in_spec(k, layer, width):
        def block(which, b):
            return jnp.where(which == k, b, jnp.where(which < k, 0, nb - 1))
        if layer is None:
            return pl.BlockSpec((rb, width), lambda which, b: (block(which, b), 0))
        return pl.BlockSpec((None, rb, width), lambda which, b: (layer, block(which, b), 0))

    in_specs = [in_spec(k, layer, cols) for k, (_, layer) in enumerate(sources)]
    args = [a for a, _ in sources]
    if row_scales is not None:
        in_specs += [in_spec(k, None, 1) for k in range(len(sources))]
        args += [v.reshape(rows, 1) for v in row_scales]
    return pl.pallas_call(
        functools.partial(_cast_kernel, n_src=len(sources), scaled=row_scales is not None),
        grid=(len(sources), nb),
        in_specs=in_specs,
        out_specs=pl.BlockSpec((None, rb, cols), lambda which, b: (which, b, 0)),
        out_shape=jax.ShapeDtypeStruct((len(sources), rows, cols), _BF16),
        compiler_params=_params(2),
        name="cast_bf16",
    )(*args)


def _glu_in_kernel(x_ref, g_ref, wa_ref, wg_ref, ba_ref, bg_ref, u_ref, xn_ref):
    @pl.when(pl.program_id(1) == 0)
    def _():
        _rmsnorm_rows(x_ref, [xn_ref], [g_ref])

    xn = xn_ref[...]
    a = jnp.dot(xn, wa_ref[...], preferred_element_type=_F32) + ba_ref[...]
    gate = jnp.dot(xn, wg_ref[...], preferred_element_type=_F32) + bg_ref[...]
    u_ref[...] = a * _sigmoid(gate)


def _glu_in(x, g, w1, b1, *, tm=1024, tn=1024):
    n, d = x.shape
    nj = d // tn
    return pl.pallas_call(
        _glu_in_kernel,
        grid=(n // tm, nj),
        in_specs=[
            pl.BlockSpec((tm, d), lambda i, j: (i, 0)),
            pl.BlockSpec((1, d), lambda i, j: (0, 0)),
            pl.BlockSpec((d, tn), lambda i, j: (0, j)),
            pl.BlockSpec((d, tn), lambda i, j: (0, j + nj)),
            pl.BlockSpec((1, tn), lambda i, j: (0, j)),
            pl.BlockSpec((1, tn), lambda i, j: (0, j + nj)),
        ],
        out_specs=pl.BlockSpec((tm, tn), lambda i, j: (i, j)),
        out_shape=jax.ShapeDtypeStruct((n, d), _F32),
        scratch_shapes=[pltpu.VMEM((tm, d), _BF16)],
        compiler_params=_params(2),
        name="glu_in",
    )(x, g, w1, w1, b1, b1)


_HALO = 32
_CONV_ROWS = 256


def _conv_out_kernel(u_ref, halo_ref, x_ref, dw_ref, dwb_ref, lng_ref, lnb_ref, w2_ref, b2_ref,
                     o_ref, buf_ref, y_ref, z_ref, *, blocks_per_seq):
    tm, d = u_ref.shape
    first = (pl.program_id(0) % blocks_per_seq) == 0
    for c in range(d // LANES):
        lanes = slice(c * LANES, (c + 1) * LANES)
        buf_ref[c, 0:_HALO, :] = jnp.where(first, 0.0, halo_ref[:, lanes])
        buf_ref[c, _HALO:, :] = u_ref[:, lanes]

    def conv_lanes(c, carry):
        lanes = pl.ds(pl.multiple_of(c * LANES, LANES), LANES)
        w = dw_ref[:, lanes]
        for r0 in range(0, tm, _CONV_ROWS):
            acc = jnp.broadcast_to(dwb_ref[:, lanes], (_CONV_ROWS, LANES))
            for k in range(CONV_WIDTH):
                shifted = buf_ref[c, pl.ds(r0 + _HALO - (CONV_WIDTH - 1) + k, _CONV_ROWS, stride=1), :]
                acc = acc + shifted * w[k:k + 1, :]
            y_ref[r0:r0 + _CONV_ROWS, lanes] = acc
        return carry

    lax.fori_loop(0, d // LANES, conv_lanes, 0)

    chunk = 256

    def norm_rows(i, carry):
        rows = pl.ds(pl.multiple_of(i * chunk, chunk), chunk)
        y = y_ref[rows, :]
        mu = jnp.mean(y, axis=-1, keepdims=True)
        yc = y - mu
        var = jnp.mean(yc * yc, axis=-1, keepdims=True)
        zn = yc * lax.rsqrt(var + LN_EPS) * lng_ref[...] + lnb_ref[...]
        z_ref[rows, :] = (zn * _sigmoid(zn)).astype(z_ref.dtype)
        return carry

    lax.fori_loop(0, tm // chunk, norm_rows, 0)

    o_ref[...] = (x_ref[...] + b2_ref[...]
                  + jnp.dot(z_ref[...], w2_ref[...], preferred_element_type=_F32))


def _conv_out(u, x, dw, dwb, lng, lnb, w2, b2, *, seq, tm=512):
    n, d = u.shape
    per_halo = tm // _HALO
    kern = functools.partial(_conv_out_kernel, blocks_per_seq=seq // tm)
    row = lambda i: (0, 0)
    return pl.pallas_call(
        kern,
        grid=(n // tm,),
        in_specs=[
            pl.BlockSpec((tm, d), lambda i: (i, 0)),
            pl.BlockSpec((_HALO, d), lambda i: (jnp.maximum(i * per_halo - 1, 0), 0)),
            pl.BlockSpec((tm, d), lambda i: (i, 0)),
            pl.BlockSpec((CONV_WIDTH, d), row),
            pl.BlockSpec((1, d), row),
            pl.BlockSpec((1, d), row),
            pl.BlockSpec((1, d), row),
            pl.BlockSpec((d, d), row, pipeline_mode=pl.Buffered(1)),
            pl.BlockSpec((1, d), row),
        ],
        out_specs=pl.BlockSpec((tm, d), lambda i: (i, 0)),
        out_shape=jax.ShapeDtypeStruct((n, d), _F32),
        scratch_shapes=[pltpu.VMEM((d // LANES, tm + _HALO, LANES), _F32),
                        pltpu.VMEM((tm, d), _F32),
                        pltpu.VMEM((tm, d), _BF16)],
        compiler_params=_params(1),
        name="conv_out",
    )(u, u, x, dw, dwb, lng, lnb, w2, b2)


def _ffn_kernel(x_ref, g_ref, wgu_hbm, wd_hbm, *rest, final_norm, tf):
    if final_norm:
        gf_ref, o_ref, xn_ref, wg_buf, wu_buf, wd_buf, sem = rest
    else:
        o_ref, xn_ref, wg_buf, wu_buf, wd_buf, sem = rest
    n_chunks = wd_hbm.shape[1] // tf
    i = pl.program_id(0)
    n_tiles = pl.num_programs(0)

    def chunk_copies(j, slot):
        cols = pl.ds(pl.multiple_of(j * tf, tf), tf)
        return (pltpu.make_async_copy(wgu_hbm.at[0, :, cols], wg_buf.at[slot], sem.at[0, slot]),
                pltpu.make_async_copy(wgu_hbm.at[1, :, cols], wu_buf.at[slot], sem.at[1, slot]),
                pltpu.make_async_copy(wd_hbm.at[0, cols, :], wd_buf.at[slot], sem.at[2, slot]))

    @pl.when(i == 0)
    def _():
        for copy in chunk_copies(0, 0):
            copy.start()

    _rmsnorm_rows(x_ref, [xn_ref], [g_ref])
    o_ref[...] = x_ref[...]

    def body(j, carry):
        count = i * n_chunks + j
        slot = count % 2
        is_last = jnp.logical_and(i == n_tiles - 1, j == n_chunks - 1)

        @pl.when(jnp.logical_not(is_last))
        def _():
            for copy in chunk_copies((j + 1) % n_chunks, 1 - slot):
                copy.start()

        for copy in chunk_copies(j, slot):
            copy.wait()
        xn = xn_ref[...]
        gate = jnp.dot(xn, wg_buf[slot], preferred_element_type=_F32)
        up = jnp.dot(xn, wu_buf[slot], preferred_element_type=_F32)
        act = (gate * _sigmoid(gate) * up).astype(_BF16)
        o_ref[...] += jnp.dot(act, wd_buf[slot], preferred_element_type=_F32)
        return carry

    lax.fori_loop(0, n_chunks, body, 0)

    if final_norm:
        _rmsnorm_rows(o_ref, [o_ref], [gf_ref])


def _ffn(x, g, w_gate_up, w_down, gf=None, *, tm=1024, tf=512):
    n, d = x.shape
    f = w_gate_up.shape[2]
    assert f % tf == 0
    final_norm = gf is not None
    in_specs = [
        pl.BlockSpec((tm, d), lambda i: (i, 0)),
        pl.BlockSpec((1, d), lambda i: (0, 0)),
        pl.BlockSpec(memory_space=pl.ANY),
        pl.BlockSpec(memory_space=pl.ANY),
    ]
    args = [x, g, w_gate_up, w_down]
    if final_norm:
        in_specs.append(pl.BlockSpec((1, d), lambda i: (0, 0)))
        args.append(gf)
    return pl.pallas_call(
        functools.partial(_ffn_kernel, final_norm=final_norm, tf=tf),
        grid=(n // tm,),
        in_specs=in_specs,
        out_specs=pl.BlockSpec((tm, d), lambda i: (i, 0)),
        out_shape=jax.ShapeDtypeStruct((n, d), _F32),
        scratch_shapes=[pltpu.VMEM((tm, d), _BF16),
                        pltpu.VMEM((2, d, tf), _BF16),
                        pltpu.VMEM((2, d, tf), _BF16),
                        pltpu.VMEM((2, tf, d), _BF16),
                        pltpu.SemaphoreType.DMA((3, 2))],
        compiler_params=_params(1),
        name="ffn_final" if final_norm else "ffn",
    )(*args)


def _kvq_kernel(x_ref, w_ref, gate_ref, up_ref, down_ref, o_ref, wgu_ref, wdn_ref, xn_ref,
                *, q_scale, blocks_per_proj, third):
    j = pl.program_id(1)
    step = pl.program_id(0) * pl.num_programs(1) + j

    @pl.when(j == 0)
    def _():
        _rmsnorm_rows(x_ref, [xn_ref], [None])

    @pl.when(j < 2 * blocks_per_proj)
    def _():
        o_ref[...] = jnp.dot(xn_ref[...], w_ref[...], preferred_element_type=_F32)

    @pl.when(j >= 2 * blocks_per_proj)
    def _():
        o_ref[...] = jnp.dot(xn_ref[...], w_ref[...], preferred_element_type=_F32) * q_scale

    @pl.when(step < third)
    def _():
        wgu_ref[...] = gate_ref[...].astype(wgu_ref.dtype)

    @pl.when(jnp.logical_and(step >= third, step < 2 * third))
    def _():
        wgu_ref[...] = up_ref[...].astype(wgu_ref.dtype)

    @pl.when(step >= 2 * third)
    def _():
        wdn_ref[...] = down_ref[...].astype(wdn_ref.dtype)


def _kvq(x, w_kvq, ffn_gate, ffn_up, ffn_down, layer, *, q_scale, tm=1024, tn=1024):
    n, d = x.shape
    f = ffn_gate.shape[2]
    per = d // tn
    steps_j = 3 * per
    third = (n // tm) * steps_j // 3
    gate_rows, down_rows = d // third, f // third
    assert third * 3 == (n // tm) * steps_j and gate_rows * third == d and down_rows * third == f
    assert gate_rows % 16 == 0 and down_rows % 16 == 0

    def block(i, j, k):
        return jnp.clip(i * steps_j + j - k * third, 0, third - 1)

    def gu_index(i, j):
        step = i * steps_j + j
        return jnp.minimum(step // third, 1), jnp.where(step < 2 * third, step % third, third - 1), 0

    return pl.pallas_call(
        functools.partial(_kvq_kernel, q_scale=q_scale, blocks_per_proj=per, third=third),
        grid=(n // tm, steps_j),
        in_specs=[
            pl.BlockSpec((tm, d), lambda i, j: (i, 0)),
            pl.BlockSpec((None, d, tn), lambda i, j: (j // per, 0, j % per)),
            pl.BlockSpec((None, gate_rows, f), lambda i, j: (layer, block(i, j, 0), 0)),
            pl.BlockSpec((None, gate_rows, f), lambda i, j: (layer, block(i, j, 1), 0)),
            pl.BlockSpec((None, down_rows, d), lambda i, j: (layer, block(i, j, 2), 0)),
        ],
        out_specs=[
            pl.BlockSpec((None, tm, tn), lambda i, j: (j // per, i, j % per)),
            pl.BlockSpec((None, gate_rows, f), gu_index),
            pl.BlockSpec((None, down_rows, d), lambda i, j: (0, block(i, j, 2), 0)),
        ],
        out_shape=[jax.ShapeDtypeStruct((3, n, d), _F32),
                   jax.ShapeDtypeStruct((2, d, f), _BF16),
                   jax.ShapeDtypeStruct((1, f, d), _BF16)],
        scratch_shapes=[pltpu.VMEM((tm, d), _BF16)],
        compiler_params=_params(2),
        name="kvq",
    )(x, w_kvq, ffn_gate, ffn_up, ffn_down)


def _step_tables():
    p = np.arange(Q_BLOCK)[:, None]
    c = np.arange(2 * Q_BLOCK)[None, :]
    pair = []
    for shift in (0, Q_BLOCK):
        j = shift + p - c
        valid = (j >= 0) & (j <= WINDOW_STEPS)
        pair.append(np.where(valid, j.astype(np.float32), np.float32(MASKED)))
    return np.stack(pair).astype(np.float32)


def _attention_kernel(slopes_ref, tab_ref, q_ref, k_ref, v_ref, o_ref,
                      bias_ref, qs_ref, ks_ref, vs_ref, t4_ref, s_ref, e_ref, rl_ref, ob_ref, lse_ref):
    seq = q_ref.shape[0]
    blocks_per_branch = seq // Q_BLOCK
    n_br = len(BRANCH_DILATIONS)
    n_groups = n_br * blocks_per_branch // GROUP
    assert blocks_per_branch % GROUP == 0 and n_groups % 2 == 0 and n_groups >= 4
    h = pl.program_id(1)
    neg_slope2 = -slopes_ref[h] * LOG2E
    for b, dil in enumerate(BRANCH_DILATIONS):
        for t in range(2):
            bias_ref[b, t] = tab_ref[t] * (dil * neg_slope2)

    ratio = DILATION_RATIO
    chunk = seq // max(BRANCH_DILATIONS)
    quarter = seq // ratio

    def finer(c):
        return pl.ds((c // ratio) + ratio * (c % ratio) * chunk, chunk, stride=ratio)

    def finer16(c):
        return pl.ds((c % ratio) * quarter + c // ratio, chunk, stride=ratio)

    srcs = (q_ref, k_ref, v_ref)
    dsts = (qs_ref, ks_ref, vs_ref)

    def regroup4(c, carry):
        dst = pl.ds(pl.multiple_of(c * chunk, chunk), chunk)
        for a in range(3):
            dsts[a][0, dst, :] = srcs[a][dst, :].astype(_BF16)
            by4 = srcs[a][finer(c), :]
            t4_ref[a, dst, :] = by4
            dsts[a][1, dst, :] = by4.astype(_BF16)
        return carry

    lax.fori_loop(0, seq // chunk, regroup4, 0)

    def regroup16(c, carry):
        dst = pl.ds(pl.multiple_of(c * chunk, chunk), chunk)
        for a in range(3):
            dsts[a][2, dst, :] = t4_ref[a, finer16(c), :].astype(_BF16)
        return carry

    lax.fori_loop(0, seq // chunk, regroup16, 0)

    def locate(g, u):
        t = g * GROUP + u
        b = t // blocks_per_branch
        jb = t % blocks_per_branch
        per_class = lax.shift_right_logical(blocks_per_branch, 2 * b)
        is_first = (jb & (per_class - 1)) == 0
        q0 = pl.multiple_of(jb * Q_BLOCK, Q_BLOCK)
        k0 = pl.multiple_of(jnp.where(is_first, q0, q0 - Q_BLOCK), Q_BLOCK)
        return b, q0, k0, jnp.where(is_first, 0, 1)

    def scores(g, slot):
        for u in range(GROUP):
            b, q0, k0, _ = locate(g, u)
            q = qs_ref[b, pl.ds(q0, Q_BLOCK), :]
            k = ks_ref[b, pl.ds(k0, 2 * Q_BLOCK), :]
            s_ref[slot, u] = lax.dot_general(q, k, (((1,), (1,)), ((), ())),
                                             preferred_element_type=_F32)

    def softmax(g, slot):
        for u in range(GROUP):
            b, q0, _, tab = locate(g, u)
            s = s_ref[slot, u] + bias_ref[b, tab]
            m = jnp.max(s, axis=-1, keepdims=True)
            e = jnp.exp2(s - m)
            l = jnp.sum(e, axis=-1, keepdims=True)
            e_ref[slot, u] = e.astype(_BF16)
            rl_ref[slot, u] = jnp.broadcast_to(1.0 / l, (Q_BLOCK, LANES))
            lse_ref[b, pl.ds(q0, Q_BLOCK), :] = jnp.broadcast_to(m + jnp.log2(l), (Q_BLOCK, LANES))

    def values(g, slot):
        for u in range(GROUP):
            b, q0, k0, _ = locate(g, u)
            v = vs_ref[b, pl.ds(k0, 2 * Q_BLOCK), :]
            acc = jnp.dot(e_ref[slot, u], v, preferred_element_type=_F32)
            ob_ref[b, pl.ds(q0, Q_BLOCK), :] = acc * rl_ref[slot, u]

    zero = jnp.int32(0)
    scores(zero, 0)
    softmax(zero, 0)
    scores(zero + 1, 1)

    def trip(i, carry):
        g = 2 * i
        values(g - 2, 0)
        softmax(g - 1, 1)
        scores(g, 0)
        values(g - 1, 1)
        softmax(g, 0)
        scores(g + 1, 1)
        return carry

    lax.fori_loop(1, n_groups // 2, trip, 0)
    values(zero + (n_groups - 2), 0)
    softmax(zero + (n_groups - 1), 1)
    values(zero + (n_groups - 1), 1)

    def coarsen(c, carry):
        src = pl.ds(pl.multiple_of(c * chunk, chunk), chunk)
        t4_ref[0, finer16(c), :] = ob_ref[2, src, :]
        t4_ref[1, finer16(c), :] = lse_ref[2, src, :]
        return carry

    lax.fori_loop(0, seq // chunk, coarsen, 0)

    rows_per = 256

    def merge(c, carry):
        by4 = pl.ds(pl.multiple_of(c * rows_per, rows_per), rows_per)
        per_class = quarter // rows_per
        natural = pl.ds((c // per_class) + ratio * (c % per_class) * rows_per, rows_per, stride=ratio)
        ls = [lse_ref[0, natural, :], lse_ref[1, by4, :], t4_ref[1, by4, :]]
        os = [ob_ref[0, natural, :], ob_ref[1, by4, :], t4_ref[0, by4, :]]
        l_max = functools.reduce(jnp.maximum, ls)
        num = 0.0
        den = 0.0
        for b in range(n_br):
            w = jnp.exp2(ls[b] - l_max)
            num = num + w * os[b]
            den = den + w
        o_ref[natural, :] = num / den
        return carry

    lax.fori_loop(0, seq // rows_per, merge, 0)


def _attention(kvq, slopes, tables, *, batch, seq):
    _, n, d = kvq.shape
    e = d // N_HEADS
    n_br = len(BRANCH_DILATIONS)
    kvq4 = kvq.reshape(3, batch, seq, d)

    def qkv_spec(which):
        return pl.BlockSpec((None, None, seq, e), lambda b, h: (which, b, 0, h))

    return pl.pallas_call(
        _attention_kernel,
        grid=(batch, N_HEADS),
        in_specs=[
            pl.BlockSpec(memory_space=pltpu.SMEM),
            pl.BlockSpec(tables.shape, lambda b, h: (0, 0, 0)),
            qkv_spec(2), qkv_spec(0), qkv_spec(1),
        ],
        out_specs=pl.BlockSpec((None, seq, e), lambda b, h: (b, 0, h)),
        out_shape=jax.ShapeDtypeStruct((batch, seq, d), _F32),
        scratch_shapes=[
            pltpu.VMEM((n_br, 2, Q_BLOCK, 2 * Q_BLOCK), _F32),
            pltpu.VMEM((n_br, seq, e), _BF16),
            pltpu.VMEM((n_br, seq, e), _BF16),
            pltpu.VMEM((n_br, seq, e), _BF16),
            pltpu.VMEM((3, seq, e), _F32),
            pltpu.VMEM((2, GROUP, Q_BLOCK, 2 * Q_BLOCK), _F32),
            pltpu.VMEM((2, GROUP, Q_BLOCK, 2 * Q_BLOCK), _BF16),
            pltpu.VMEM((2, GROUP, Q_BLOCK, LANES), _F32),
            pltpu.VMEM((n_br, seq, e), _F32),
            pltpu.VMEM((n_br, seq, LANES), _F32),
        ],
        compiler_params=_params(2),
        name="dilated_attention",
    )(slopes, tables, kvq4, kvq4, kvq4)


def _out_proj_kernel(a_ref, x_ref, w_ref, o_ref):
    o_ref[...] = x_ref[...] + jnp.dot(a_ref[...].astype(_BF16), w_ref[...],
                                      preferred_element_type=_F32)


def _out_proj(att, x, w, *, tm=512):
    n, d = x.shape
    return pl.pallas_call(
        _out_proj_kernel,
        grid=(n // tm,),
        in_specs=[
            pl.BlockSpec((tm, d), lambda i: (i, 0)),
            pl.BlockSpec((tm, d), lambda i: (i, 0)),
            pl.BlockSpec((d, d), lambda i: (0, 0)),
        ],
        out_specs=pl.BlockSpec((tm, d), lambda i: (i, 0)),
        out_shape=jax.ShapeDtypeStruct((n, d), _F32),
        compiler_params=_params(1),
        name="out_proj",
    )(att, x, w)


def kernel(x, a_norm_g, conv_w1, conv_b1, conv_dw, conv_dw_b, conv_ln_g, conv_ln_b, conv_w2, conv_b2,
           kv_norm_g, w_k, w_v, b_norm_g, w_q, w_o, ffn_norm_g, ffn_w_gate, ffn_w_up, ffn_w_down,
           final_norm_g):
    batch, seq, d = x.shape
    n_a = a_norm_g.shape[0]
    n_b = b_norm_g.shape[0]
    depth = n_a + n_b
    assert d % N_HEADS == 0 and d // N_HEADS == LANES
    assert BRANCH_DILATIONS == (1, 4, 16) and seq % (max(BRANCH_DILATIONS) * Q_BLOCK * 2) == 0

    row = lambda v: v.reshape(1, -1)
    h = x.reshape(batch * seq, d)
    head_dim = d // N_HEADS
    slopes = jnp.exp2(-8.0 * jnp.arange(1, N_HEADS + 1, dtype=_F32) / N_HEADS)
    tables = jnp.asarray(_step_tables())
    kvq = None

    for layer in range(depth):
        ffn_weights = None
        if layer < n_a:
            a = layer
            w1 = _cast_stack([(conv_w1, a)])[0]
            w2 = _cast_stack([(conv_w2, a)])[0]
            u = _glu_in(h, row(a_norm_g[a]), w1, row(conv_b1[a]))
            h = _conv_out(u, h, conv_dw[a], row(conv_dw_b[a]), row(conv_ln_g[a]), row(conv_ln_b[a]),
                          w2, row(conv_b2[a]), seq=seq)
        else:
            i = layer - n_a
            assert n_b == 1, "one attention layer supported"
            w_kvq = _cast_stack([(w_k, None), (w_v, None), (w_q, i)],
                                row_scales=[kv_norm_g, kv_norm_g, b_norm_g[i]])
            kvq, *ffn_weights = _kvq(h, w_kvq, ffn_w_gate, ffn_w_up, ffn_w_down, layer,
                                     q_scale=head_dim ** -0.5 * LOG2E)
            att = _attention(kvq, slopes, tables, batch=batch, seq=seq)
            h = _out_proj(att.reshape(batch * seq, d), h, _cast_stack([(w_o, i)])[0])
        last = layer == depth - 1
        if ffn_weights is None:
            ffn_weights = (_cast_stack([(ffn_w_gate, layer), (ffn_w_up, layer)]),
                           _cast_stack([(ffn_w_down, layer)]))
        h = _ffn(h, row(ffn_norm_g[layer]), *ffn_weights, row(final_norm_g) if last else None)
    return h.reshape(batch, seq, d)
```

```python
import functools
import math

import numpy as np
import jax
import jax.numpy as jnp
from jax import lax
from jax.experimental import pallas as pl
from jax.experimental.pallas import tpu as pltpu

RMS_EPS = 1e-6
LN_EPS = 1e-5
N_HEADS = 16
CONV_WIDTH = 31
BRANCH_DILATIONS = (1, 4, 16)
DILATION_RATIO = 4
WINDOW_STEPS = 128
Q_BLOCK = 128
GROUP = 4
MASKED = 1e30
LOG2E = math.log2(math.e)
LANES = 128

_VMEM_LIMIT = 56 * 1024 * 1024
_BF16 = jnp.bfloat16
_F32 = jnp.float32


def _params(n_axes):
    return pltpu.CompilerParams(dimension_semantics=("arbitrary",) * n_axes,
                                vmem_limit_bytes=_VMEM_LIMIT)


def _sigmoid(x):
    return 1.0 / (1.0 + jnp.exp(-x))


def _rmsnorm_rows(x_ref, out_refs, g_refs, chunk=256):
    n = x_ref.shape[0] // chunk

    def body(i, carry):
        r = pl.multiple_of(i * chunk, chunk)
        x = x_ref[pl.ds(r, chunk), :]
        xhat = x * lax.rsqrt(jnp.mean(x * x, axis=-1, keepdims=True) + RMS_EPS)
        for o_ref, g_ref in zip(out_refs, g_refs):
            y = xhat if g_ref is None else xhat * g_ref[...]
            o_ref[pl.ds(r, chunk), :] = y.astype(o_ref.dtype)
        return carry

    lax.fori_loop(0, n, body, 0)


def _cast_kernel(*refs, n_src, scaled):
    src_refs, scale_refs, o_ref = refs[:n_src], refs[n_src:-1], refs[-1]
    which = pl.program_id(0)
    for k, src_ref in enumerate(src_refs):
        @pl.when(which == k)
        def _(k=k, src_ref=src_ref):
            w = src_ref[...]
            if scaled:
                w = w * scale_refs[k][...]
            o_ref[...] = w.astype(o_ref.dtype)


def _cast_stack(sources, row_scales=None):
    rows, cols = sources[0][0].shape[-2:]
    block_elems = (1 << 21) if len(sources) <= 2 else (1 << 20)
    rb = 1 << ((block_elems // cols).bit_length() - 1)
    while rows % rb:
        rb //= 2
    assert rb >= 16
    nb = rows // rb

    def in_spec(k, layer, width):
        def block(which, b):
            return jnp.where(which == k, b, jnp.where(which < k, 0, nb - 1))
        if layer is None:
            return pl.BlockSpec((rb, width), lambda which, b: (block(which, b), 0))
        return pl.BlockSpec((None, rb, width), lambda which, b: (layer, block(which, b), 0))

    in_specs = [in_spec(k, layer, cols) for k, (_, layer) in enumerate(sources)]
    args = [a for a, _ in sources]
    if row_scales is not None:
        in_specs += [in_spec(k, None, 1) for k in range(len(sources))]
        args += [v.reshape(rows, 1) for v in row_scales]
    return pl.pallas_call(
        functools.partial(_cast_kernel, n_src=len(sources), scaled=row_scales is not None),
        grid=(len(sources), nb),
        in_specs=in_specs,
        out_specs=pl.BlockSpec((None, rb, cols), lambda which, b: (which, b, 0)),
        out_shape=jax.ShapeDtypeStruct((len(sources), rows, cols), _BF16),
        compiler_params=_params(2),
        name="cast_bf16",
    )(*args)


def _glu_in_kernel(x_ref, g_ref, wa_ref, wg_ref, ba_ref, bg_ref, u_ref, xn_ref):
    @pl.when(pl.program_id(1) == 0)
    def _():
        _rmsnorm_rows(x_ref, [xn_ref], [g_ref])

    xn = xn_ref[...]
    a = jnp.dot(xn, wa_ref[...], preferred_element_type=_F32) + ba_ref[...]
    gate = jnp.dot(xn, wg_ref[...], preferred_element_type=_F32) + bg_ref[...]
    u_ref[...] = a * _sigmoid(gate)


def _glu_in(x, g, w1, b1, *, tm=1024, tn=1024):
    n, d = x.shape
    nj = d // tn
    return pl.pallas_call(
        _glu_in_kernel,
        grid=(n // tm, nj),
        in_specs=[
            pl.BlockSpec((tm, d), lambda i, j: (i, 0)),
            pl.BlockSpec((1, d), lambda i, j: (0, 0)),
            pl.BlockSpec((d, tn), lambda i, j: (0, j)),
            pl.BlockSpec((d, tn), lambda i, j: (0, j + nj)),
            pl.BlockSpec((1, tn), lambda i, j: (0, j)),
            pl.BlockSpec((1, tn), lambda i, j: (0, j + nj)),
        ],
        out_specs=pl.BlockSpec((tm, tn), lambda i, j: (i, j)),
        out_shape=jax.ShapeDtypeStruct((n, d), _F32),
        scratch_shapes=[pltpu.VMEM((tm, d), _BF16)],
        compiler_params=_params(2),
        name="glu_in",
    )(x, g, w1, w1, b1, b1)


_HALO = 32
_CONV_ROWS = 256


def _conv_out_kernel(u_ref, halo_ref, x_ref, dw_ref, dwb_ref, lng_ref, lnb_ref, w2_ref, b2_ref,
                     o_ref, buf_ref, y_ref, z_ref, *, blocks_per_seq):
    tm, d = u_ref.shape
    first = (pl.program_id(0) % blocks_per_seq) == 0
    for c in range(d // LANES):
        lanes = slice(c * LANES, (c + 1) * LANES)
        buf_ref[c, 0:_HALO, :] = jnp.where(first, 0.0, halo_ref[:, lanes])
        buf_ref[c, _HALO:, :] = u_ref[:, lanes]

    def conv_lanes(c, carry):
        lanes = pl.ds(pl.multiple_of(c * LANES, LANES), LANES)
        w = dw_ref[:, lanes]
        for r0 in range(0, tm, _CONV_ROWS):
            acc = jnp.broadcast_to(dwb_ref[:, lanes], (_CONV_ROWS, LANES))
            for k in range(CONV_WIDTH):
                shifted = buf_ref[c, pl.ds(r0 + _HALO - (CONV_WIDTH - 1) + k, _CONV_ROWS, stride=1), :]
                acc = acc + shifted * w[k:k + 1, :]
            y_ref[r0:r0 + _CONV_ROWS, lanes] = acc
        return carry

    lax.fori_loop(0, d // LANES, conv_lanes, 0)

    chunk = 256

    def norm_rows(i, carry):
        rows = pl.ds(pl.multiple_of(i * chunk, chunk), chunk)
        y = y_ref[rows, :]
        mu = jnp.mean(y, axis=-1, keepdims=True)
        yc = y - mu
        var = jnp.mean(yc * yc, axis=-1, keepdims=True)
        zn = yc * lax.rsqrt(var + LN_EPS) * lng_ref[...] + lnb_ref[...]
        z_ref[rows, :] = (zn * _sigmoid(zn)).astype(z_ref.dtype)
        return carry

    lax.fori_loop(0, tm // chunk, norm_rows, 0)

    o_ref[...] = (x_ref[...] + b2_ref[...]
                  + jnp.dot(z_ref[...], w2_ref[...], preferred_element_type=_F32))


def _conv_out(u, x, dw, dwb, lng, lnb, w2, b2, *, seq, tm=512):
    n, d = u.shape
    per_halo = tm // _HALO
    kern = functools.partial(_conv_out_kernel, blocks_per_seq=seq // tm)
    row = lambda i: (0, 0)
    return pl.pallas_call(
        kern,
        grid=(n // tm,),
        in_specs=[
            pl.BlockSpec((tm, d), lambda i: (i, 0)),
            pl.BlockSpec((_HALO, d), lambda i: (jnp.maximum(i * per_halo - 1, 0), 0)),
            pl.BlockSpec((tm, d), lambda i: (i, 0)),
            pl.BlockSpec((CONV_WIDTH, d), row),
            pl.BlockSpec((1, d), row),
            pl.BlockSpec((1, d), row),
            pl.BlockSpec((1, d), row),
            pl.BlockSpec((d, d), row, pipeline_mode=pl.Buffered(1)),
            pl.BlockSpec((1, d), row),
        ],
        out_specs=pl.BlockSpec((tm, d), lambda i: (i, 0)),
        out_shape=jax.ShapeDtypeStruct((n, d), _F32),
        scratch_shapes=[pltpu.VMEM((d // LANES, tm + _HALO, LANES), _F32),
                        pltpu.VMEM((tm, d), _F32),
                        pltpu.VMEM((tm, d), _BF16)],
        compiler_params=_params(1),
        name="conv_out",
    )(u, u, x, dw, dwb, lng, lnb, w2, b2)


def _ffn_kernel(x_ref, g_ref, wgu_hbm, wd_hbm, *rest, final_norm, tf):
    if final_norm:
        gf_ref, o_ref, xn_ref, wg_buf, wu_buf, wd_buf, sem = rest
    else:
        o_ref, xn_ref, wg_buf, wu_buf, wd_buf, sem = rest
    n_chunks = wd_hbm.shape[1] // tf
    i = pl.program_id(0)
    n_tiles = pl.num_programs(0)

    def chunk_copies(j, slot):
        cols = pl.ds(pl.multiple_of(j * tf, tf), tf)
        return (pltpu.make_async_copy(wgu_hbm.at[0, :, cols], wg_buf.at[slot], sem.at[0, slot]),
                pltpu.make_async_copy(wgu_hbm.at[1, :, cols], wu_buf.at[slot], sem.at[1, slot]),
                pltpu.make_async_copy(wd_hbm.at[0, cols, :], wd_buf.at[slot], sem.at[2, slot]))

    @pl.when(i == 0)
    def _():
        for copy in chunk_copies(0, 0):
            copy.start()

    _rmsnorm_rows(x_ref, [xn_ref], [g_ref])
    o_ref[...] = x_ref[...]

    def body(j, carry):
        count = i * n_chunks + j
        slot = count % 2
        is_last = jnp.logical_and(i == n_tiles - 1, j == n_chunks - 1)

        @pl.when(jnp.logical_not(is_last))
        def _():
            for copy in chunk_copies((j + 1) % n_chunks, 1 - slot):
                copy.start()

        for copy in chunk_copies(j, slot):
            copy.wait()
        xn = xn_ref[...]
        gate = jnp.dot(xn, wg_buf[slot], preferred_element_type=_F32)
        up = jnp.dot(xn, wu_buf[slot], preferred_element_type=_F32)
        act = (gate * _sigmoid(gate) * up).astype(_BF16)
        o_ref[...] += jnp.dot(act, wd_buf[slot], preferred_element_type=_F32)
        return carry

    lax.fori_loop(0, n_chunks, body, 0)

    if final_norm:
        _rmsnorm_rows(o_ref, [o_ref], [gf_ref])


def _ffn(x, g, w_gate_up, w_down, gf=None, *, tm=1024, tf=512):
    n, d = x.shape
    f = w_gate_up.shape[2]
    assert f % tf == 0
    final_norm = gf is not None
    in_specs = [
        pl.BlockSpec((tm, d), lambda i: (i, 0)),
        pl.BlockSpec((1, d), lambda i: (0, 0)),
        pl.BlockSpec(memory_space=pl.ANY),
        pl.BlockSpec(memory_space=pl.ANY),
    ]
    args = [x, g, w_gate_up, w_down]
    if final_norm:
        in_specs.append(pl.BlockSpec((1, d), lambda i: (0, 0)))
        args.append(gf)
    return pl.pallas_call(
        functools.partial(_ffn_kernel, final_norm=final_norm, tf=tf),
        grid=(n // tm,),
        in_specs=in_specs,
        out_specs=pl.BlockSpec((tm, d), lambda i: (i, 0)),
        out_shape=jax.ShapeDtypeStruct((n, d), _F32),
        scratch_shapes=[pltpu.VMEM((tm, d), _BF16),
                        pltpu.VMEM((2, d, tf), _BF16),
                        pltpu.VMEM((2, d, tf), _BF16),
                        pltpu.VMEM((2, tf, d), _BF16),
                        pltpu.SemaphoreType.DMA((3, 2))],
        compiler_params=_params(1),
        name="ffn_final" if final_norm else "ffn",
    )(*args)


def _kvq_kernel(x_ref, w_ref, gate_ref, up_ref, down_ref, o_ref, wgu_ref, wdn_ref, xn_ref,
                *, q_scale, blocks_per_proj, third):
    j = pl.program_id(1)
    step = pl.program_id(0) * pl.num_programs(1) + j

    @pl.when(j == 0)
    def _():
        _rmsnorm_rows(x_ref, [xn_ref], [None])

    @pl.when(j < 2 * blocks_per_proj)
    def _():
        o_ref[...] = jnp.dot(xn_ref[...], w_ref[...], preferred_element_type=_F32)

    @pl.when(j >= 2 * blocks_per_proj)
    def _():
        o_ref[...] = jnp.dot(xn_ref[...], w_ref[...], preferred_element_type=_F32) * q_scale

    @pl.when(step < third)
    def _():
        wgu_ref[...] = gate_ref[...].astype(wgu_ref.dtype)

    @pl.when(jnp.logical_and(step >= third, step < 2 * third))
    def _():
        wgu_ref[...] = up_ref[...].astype(wgu_ref.dtype)

    @pl.when(step >= 2 * third)
    def _():
        wdn_ref[...] = down_ref[...].astype(wdn_ref.dtype)


def _kvq(x, w_kvq, ffn_gate, ffn_up, ffn_down, layer, *, q_scale, tm=1024, tn=1024):
    n, d = x.shape
    f = ffn_gate.shape[2]
    per = d // tn
    steps_j = 3 * per
    third = (n // tm) * steps_j // 3
    gate_rows, down_rows = d // third, f // third
    assert third * 3 == (n // tm) * steps_j and gate_rows * third == d and down_rows * third == f
    assert gate_rows % 16 == 0 and down_rows % 16 == 0

    def block(i, j, k):
        return jnp.clip(i * steps_j + j - k * third, 0, third - 1)

    def gu_index(i, j):
        step = i * steps_j + j
        return jnp.minimum(step // third, 1), jnp.where(step < 2 * third, step % third, third - 1), 0

    return pl.pallas_call(
        functools.partial(_kvq_kernel, q_scale=q_scale, blocks_per_proj=per, third=third),
        grid=(n // tm, steps_j),
        in_specs=[
            pl.BlockSpec((tm, d), lambda i, j: (i, 0)),
            pl.BlockSpec((None, d, tn), lambda i, j: (j // per, 0, j % per)),
            pl.BlockSpec((None, gate_rows, f), lambda i, j: (layer, block(i, j, 0), 0)),
            pl.BlockSpec((None, gate_rows, f), lambda i, j: (layer, block(i, j, 1), 0)),
            pl.BlockSpec((None, down_rows, d), lambda i, j: (layer, block(i, j, 2), 0)),
        ],
        out_specs=[
            pl.BlockSpec((None, tm, tn), lambda i, j: (j // per, i, j % per)),
            pl.BlockSpec((None, gate_rows, f), gu_index),
            pl.BlockSpec((None, down_rows, d), lambda i, j: (0, block(i, j, 2), 0)),
        ],
        out_shape=[jax.ShapeDtypeStruct((3, n, d), _F32),
                   jax.ShapeDtypeStruct((2, d, f), _BF16),
                   jax.ShapeDtypeStruct((1, f, d), _BF16)],
        scratch_shapes=[pltpu.VMEM((tm, d), _BF16)],
        compiler_params=_params(2),
        name="kvq",
    )(x, w_kvq, ffn_gate, ffn_up, ffn_down)


def _step_tables():
    p = np.arange(Q_BLOCK)[:, None]
    c = np.arange(2 * Q_BLOCK)[None, :]
    pair = []
    for shift in (0, Q_BLOCK):
        j = shift + p - c
        valid = (j >= 0) & (j <= WINDOW_STEPS)
        pair.append(np.where(valid, j.astype(np.float32), np.float32(MASKED)))
    return np.stack(pair).astype(np.float32)


def _attention_kernel(slopes_ref, tab_ref, q_ref, k_ref, v_ref, o_ref,
                      bias_ref, qs_ref, ks_ref, vs_ref, t4_ref, s_ref, e_ref, rl_ref, ob_ref, lse_ref):
    seq = q_ref.shape[0]
    blocks_per_branch = seq // Q_BLOCK
    n_br = len(BRANCH_DILATIONS)
    n_groups = n_br * blocks_per_branch // GROUP
    assert blocks_per_branch % GROUP == 0 and n_groups % 2 == 0 and n_groups >= 4
    h = pl.program_id(1)
    neg_slope2 = -slopes_ref[h] * LOG2E
    for b, dil in enumerate(BRANCH_DILATIONS):
        for t in range(2):
            bias_ref[b, t] = tab_ref[t] * (dil * neg_slope2)

    ratio = DILATION_RATIO
    chunk = seq // max(BRANCH_DILATIONS)
    quarter = seq // ratio

    def finer(c):
        return pl.ds((c // ratio) + ratio * (c % ratio) * chunk, chunk, stride=ratio)

    def finer16(c):
        return pl.ds((c % ratio) * quarter + c // ratio, chunk, stride=ratio)

    srcs = (q_ref, k_ref, v_ref)
    dsts = (qs_ref, ks_ref, vs_ref)

    def regroup4(c, carry):
        dst = pl.ds(pl.multiple_of(c * chunk, chunk), chunk)
        for a in range(3):
            dsts[a][0, dst, :] = srcs[a][dst, :].astype(_BF16)
            by4 = srcs[a][finer(c), :]
            t4_ref[a, dst, :] = by4
            dsts[a][1, dst, :] = by4.astype(_BF16)
        return carry

    lax.fori_loop(0, seq // chunk, regroup4, 0)

    def regroup16(c, carry):
        dst = pl.ds(pl.multiple_of(c * chunk, chunk), chunk)
        for a in range(3):
            dsts[a][2, dst, :] = t4_ref[a, finer16(c), :].astype(_BF16)
        return carry

    lax.fori_loop(0, seq // chunk, regroup16, 0)

    def locate(g, u):
        t = g * GROUP + u
        b = t // blocks_per_branch
        jb = t % blocks_per_branch
        per_class = lax.shift_right_logical(blocks_per_branch, 2 * b)
        is_first = (jb & (per_class - 1)) == 0
        q0 = pl.multiple_of(jb * Q_BLOCK, Q_BLOCK)
        k0 = pl.multiple_of(jnp.where(is_first, q0, q0 - Q_BLOCK), Q_BLOCK)
        return b, q0, k0, jnp.where(is_first, 0, 1)

    def scores(g, slot):
        for u in range(GROUP):
            b, q0, k0, _ = locate(g, u)
            q = qs_ref[b, pl.ds(q0, Q_BLOCK), :]
            k = ks_ref[b, pl.ds(k0, 2 * Q_BLOCK), :]
            s_ref[slot, u] = lax.dot_general(q, k, (((1,), (1,)), ((), ())),
                                             preferred_element_type=_F32)

    def softmax(g, slot):
        for u in range(GROUP):
            b, q0, _, tab = locate(g, u)
            s = s_ref[slot, u] + bias_ref[b, tab]
            m = jnp.max(s, axis=-1, keepdims=True)
            e = jnp.exp2(s - m)
            l = jnp.sum(e, axis=-1, keepdims=True)
            e_ref[slot, u] = e.astype(_BF16)
            rl_ref[slot, u] = jnp.broadcast_to(1.0 / l, (Q_BLOCK, LANES))
            lse_ref[b, pl.ds(q0, Q_BLOCK), :] = jnp.broadcast_to(m + jnp.log2(l), (Q_BLOCK, LANES))

    def values(g, slot):
        for u in range(GROUP):
            b, q0, k0, _ = locate(g, u)
            v = vs_ref[b, pl.ds(k0, 2 * Q_BLOCK), :]
            acc = jnp.dot(e_ref[slot, u], v, preferred_element_type=_F32)
            ob_ref[b, pl.ds(q0, Q_BLOCK), :] = acc * rl_ref[slot, u]

    zero = jnp.int32(0)
    scores(zero, 0)
    softmax(zero, 0)
    scores(zero + 1, 1)

    def trip(i, carry):
        g = 2 * i
        values(g - 2, 0)
        softmax(g - 1, 1)
        scores(g, 0)
        values(g - 1, 1)
        softmax(g, 0)
        scores(g + 1, 1)
        return carry

    lax.fori_loop(1, n_groups // 2, trip, 0)
    values(zero + (n_groups - 2), 0)
    softmax(zero + (n_groups - 1), 1)
    values(zero + (n_groups - 1), 1)

    def coarsen(c, carry):
        src = pl.ds(pl.multiple_of(c * chunk, chunk), chunk)
        t4_ref[0, finer16(c), :] = ob_ref[2, src, :]
        t4_ref[1, finer16(c), :] = lse_ref[2, src, :]
        return carry

    lax.fori_loop(0, seq // chunk, coarsen, 0)

    rows_per = 512

    def merge(c, carry):
        by4 = pl.ds(pl.multiple_of(c * rows_per, rows_per), rows_per)
        per_class = quarter // rows_per
        natural = pl.ds((c // per_class) + ratio * (c % per_class) * rows_per, rows_per, stride=ratio)
        ls = [lse_ref[0, natural, :], lse_ref[1, by4, :], t4_ref[1, by4, :]]
        os = [ob_ref[0, natural, :], ob_ref[1, by4, :], t4_ref[0, by4, :]]
        l_max = functools.reduce(jnp.maximum, ls)
        num = 0.0
        den = 0.0
        for b in range(n_br):
            w = jnp.exp2(ls[b] - l_max)
            num = num + w * os[b]
            den = den + w
        o_ref[natural, :] = num / den
        return carry

    lax.fori_loop(0, seq // rows_per, merge, 0)


def _attention(kvq, slopes, tables, *, batch, seq):
    _, n, d = kvq.shape
    e = d // N_HEADS
    n_br = len(BRANCH_DILATIONS)
    kvq4 = kvq.reshape(3, batch, seq, d)

    def qkv_spec(which):
        return pl.BlockSpec((None, None, seq, e), lambda b, h: (which, b, 0, h))

    return pl.pallas_call(
        _attention_kernel,
        grid=(batch, N_HEADS),
        in_specs=[
            pl.BlockSpec(memory_space=pltpu.SMEM),
            pl.BlockSpec(tables.shape, lambda b, h: (0, 0, 0)),
            qkv_spec(2), qkv_spec(0), qkv_spec(1),
        ],
        out_specs=pl.BlockSpec((None, seq, e), lambda b, h: (b, 0, h)),
        out_shape=jax.ShapeDtypeStruct((batch, seq, d), _F32),
        scratch_shapes=[
            pltpu.VMEM((n_br, 2, Q_BLOCK, 2 * Q_BLOCK), _F32),
            pltpu.VMEM((n_br, seq, e), _BF16),
            pltpu.VMEM((n_br, seq, e), _BF16),
            pltpu.VMEM((n_br, seq, e), _BF16),
            pltpu.VMEM((3, seq, e), _F32),
            pltpu.VMEM((2, GROUP, Q_BLOCK, 2 * Q_BLOCK), _F32),
            pltpu.VMEM((2, GROUP, Q_BLOCK, 2 * Q_BLOCK), _BF16),
            pltpu.VMEM((2, GROUP, Q_BLOCK, LANES), _F32),
            pltpu.VMEM((n_br, seq, e), _F32),
            pltpu.VMEM((n_br, seq, LANES), _F32),
        ],
        compiler_params=_params(2),
        name="dilated_attention",
    )(slopes, tables, kvq4, kvq4, kvq4)


def _out_proj_kernel(a_ref, x_ref, w_ref, o_ref):
    o_ref[...] = x_ref[...] + jnp.dot(a_ref[...].astype(_BF16), w_ref[...],
                                      preferred_element_type=_F32)


def _out_proj(att, x, w, *, tm=512):
    n, d = x.shape
    return pl.pallas_call(
        _out_proj_kernel,
        grid=(n // tm,),
        in_specs=[
            pl.BlockSpec((tm, d), lambda i: (i, 0)),
            pl.BlockSpec((tm, d), lambda i: (i, 0)),
            pl.BlockSpec((d, d), lambda i: (0, 0)),
        ],
        out_specs=pl.BlockSpec((tm, d), lambda i: (i, 0)),
        out_shape=jax.ShapeDtypeStruct((n, d), _F32),
        compiler_params=_params(1),
        name="out_proj",
    )(att, x, w)


def kernel(x, a_norm_g, conv_w1, conv_b1, conv_dw, conv_dw_b, conv_ln_g, conv_ln_b, conv_w2, conv_b2,
           kv_norm_g, w_k, w_v, b_norm_g, w_q, w_o, ffn_norm_g, ffn_w_gate, ffn_w_up, ffn_w_down,
           final_norm_g):
    batch, seq, d = x.shape
    n_a = a_norm_g.shape[0]
    n_b = b_norm_g.shape[0]
    depth = n_a + n_b
    assert d % N_HEADS == 0 and d // N_HEADS == LANES
    assert BRANCH_DILATIONS == (1, 4, 16) and seq % (max(BRANCH_DILATIONS) * Q_BLOCK * 2) == 0

    row = lambda v: v.reshape(1, -1)
    h = x.reshape(batch * seq, d)
    head_dim = d // N_HEADS
    slopes = jnp.exp2(-8.0 * jnp.arange(1, N_HEADS + 1, dtype=_F32) / N_HEADS)
    tables = jnp.asarray(_step_tables())
    kvq = None

    for layer in range(depth):
        ffn_weights = None
        if layer < n_a:
            a = layer
            w1 = _cast_stack([(conv_w1, a)])[0]
            w2 = _cast_stack([(conv_w2, a)])[0]
            u = _glu_in(h, row(a_norm_g[a]), w1, row(conv_b1[a]))
            h = _conv_out(u, h, conv_dw[a], row(conv_dw_b[a]), row(conv_ln_g[a]), row(conv_ln_b[a]),
                          w2, row(conv_b2[a]), seq=seq)
        else:
            i = layer - n_a
            assert n_b == 1, "one attention layer supported"
            w_kvq = _cast_stack([(w_k, None), (w_v, None), (w_q, i)],
                                row_scales=[kv_norm_g, kv_norm_g, b_norm_g[i]])
            kvq, *ffn_weights = _kvq(h, w_kvq, ffn_w_gate, ffn_w_up, ffn_w_down, layer,
                                     q_scale=head_dim ** -0.5 * LOG2E)
            att = _attention(kvq, slopes, tables, batch=batch, seq=seq)
            h = _out_proj(att.reshape(batch * seq, d), h, _cast_stack([(w_o, i)])[0])
        last = layer == depth - 1
        if ffn_weights is None:
            ffn_weights = (_cast_stack([(ffn_w_gate, layer), (ffn_w_up, layer)]),
                           _cast_stack([(ffn_w_down, layer)]))
        h = _ffn(h, row(ffn_norm_g[layer]), *ffn_weights, row(final_norm_g) if last else None)
    return h.reshape(batch, seq, d)
```

```python
import functools
import math

import numpy as np
import jax
import jax.numpy as jnp
from jax import lax
from jax.experimental import pallas as pl
from jax.experimental.pallas import tpu as pltpu

RMS_EPS = 1e-6
LN_EPS = 1e-5
N_HEADS = 16
CONV_WIDTH = 31
BRANCH_DILATIONS = (1, 4, 16)
DILATION_RATIO = 4
WINDOW_STEPS = 128
Q_BLOCK = 128
GROUP = 4
MASKED = 1e30
LOG2E = math.log2(math.e)
LANES = 128

_VMEM_LIMIT = 56 * 1024 * 1024
_BF16 = jnp.bfloat16
_F32 = jnp.float32


def _params(n_axes):
    return pltpu.CompilerParams(dimension_semantics=("arbitrary",) * n_axes,
                                vmem_limit_bytes=_VMEM_LIMIT)


def _sigmoid(x):
    return 1.0 / (1.0 + jnp.exp(-x))


def _rmsnorm_rows(x_ref, out_refs, g_refs, chunk=256):
    n = x_ref.shape[0] // chunk

    def body(i, carry):
        r = pl.multiple_of(i * chunk, chunk)
        x = x_ref[pl.ds(r, chunk), :]
        xhat = x * lax.rsqrt(jnp.mean(x * x, axis=-1, keepdims=True) + RMS_EPS)
        for o_ref, g_ref in zip(out_refs, g_refs):
            y = xhat if g_ref is None else xhat * g_ref[...]
            o_ref[pl.ds(r, chunk), :] = y.astype(o_ref.dtype)
        return carry

    lax.fori_loop(0, n, body, 0)


def _cast_kernel(*refs, n_src, scaled):
    src_refs, scale_refs, o_ref = refs[:n_src], refs[n_src:-1], refs[-1]
    which = pl.program_id(0)
    for k, src_ref in enumerate(src_refs):
        @pl.when(which == k)
        def _(k=k, src_ref=src_ref):
            w = src_ref[...]
            if scaled:
                w = w * scale_refs[k][...]
            o_ref[...] = w.astype(o_ref.dtype)


def _cast_stack(sources, row_scales=None):
    rows, cols = sources[0][0].shape[-2:]
    block_elems = (1 << 21) if len(sources) <= 2 else (1 << 20)
    rb = 1 << ((block_elems // cols).bit_length() - 1)
    while rows % rb:
        rb //= 2
    assert rb >= 16
    nb = rows // rb

    def in_spec(k, layer, width):
        def block(which, b):
            return jnp.where(which == k, b, jnp.where(which < k, 0, nb - 1))
        if layer is None:
            return pl.BlockSpec((rb, width), lambda which, b: (block(which, b), 0))
        return pl.BlockSpec((None, rb, width), lambda which, b: (layer, block(which, b), 0))

    in_specs = [in_spec(k, layer, cols) for k, (_, layer) in enumerate(sources)]
    args = [a for a, _ in sources]
    if row_scales is not None:
        in_specs += [in_spec(k, None, 1) for k in range(len(sources))]
        args += [v.reshape(rows, 1) for v in row_scales]
    return pl.pallas_call(
        functools.partial(_cast_kernel, n_src=len(sources), scaled=row_scales is not None),
        grid=(len(sources), nb),
        in_specs=in_specs,
        out_specs=pl.BlockSpec((None, rb, cols), lambda which, b: (which, b, 0)),
        out_shape=jax.ShapeDtypeStruct((len(sources), rows, cols), _BF16),
        compiler_params=_params(2),
        name="cast_bf16",
    )(*args)


def _glu_in_kernel(x_ref, g_ref, wa_ref, wg_ref, ba_ref, bg_ref, u_ref, xn_ref):
    @pl.when(pl.program_id(1) == 0)
    def _():
        _rmsnorm_rows(x_ref, [xn_ref], [g_ref])

    xn = xn_ref[...]
    a = jnp.dot(xn, wa_ref[...], preferred_element_type=_F32) + ba_ref[...]
    gate = jnp.dot(xn, wg_ref[...], preferred_element_type=_F32) + bg_ref[...]
    u_ref[...] = a * _sigmoid(gate)


def _glu_in(x, g, w1, b1, *, tm=1024, tn=1024):
    n, d = x.shape
    nj = d // tn
    return pl.pallas_call(
        _glu_in_kernel,
        grid=(n // tm, nj),
        in_specs=[
            pl.BlockSpec((tm, d), lambda i, j: (i, 0)),
            pl.BlockSpec((1, d), lambda i, j: (0, 0)),
            pl.BlockSpec((d, tn), lambda i, j: (0, j)),
            pl.BlockSpec((d, tn), lambda i, j: (0, j + nj)),
            pl.BlockSpec((1, tn), lambda i, j: (0, j)),
            pl.BlockSpec((1, tn), lambda i, j: (0, j + nj)),
        ],
        out_specs=pl.BlockSpec((tm, tn), lambda i, j: (i, j)),
        out_shape=jax.ShapeDtypeStruct((n, d), _F32),
        scratch_shapes=[pltpu.VMEM((tm, d), _BF16)],
        compiler_params=_params(2),
        name="glu_in",
    )(x, g, w1, w1, b1, b1)


_HALO = 32
_CONV_ROWS = 256


def _conv_out_kernel(u_ref, halo_ref, x_ref, dw_ref, dwb_ref, lng_ref, lnb_ref, w2_ref, b2_ref,
                     o_ref, buf_ref, y_ref, z_ref, *, blocks_per_seq):
    tm, d = u_ref.shape
    first = (pl.program_id(0) % blocks_per_seq) == 0
    for c in range(d // LANES):
        lanes = slice(c * LANES, (c + 1) * LANES)
        buf_ref[c, 0:_HALO, :] = jnp.where(first, 0.0, halo_ref[:, lanes])
        buf_ref[c, _HALO:, :] = u_ref[:, lanes]

    def conv_lanes(c, carry):
        lanes = pl.ds(pl.multiple_of(c * LANES, LANES), LANES)
        w = dw_ref[:, lanes]
        for r0 in range(0, tm, _CONV_ROWS):
            acc = jnp.broadcast_to(dwb_ref[:, lanes], (_CONV_ROWS, LANES))
            for k in range(CONV_WIDTH):
                shifted = buf_ref[c, pl.ds(r0 + _HALO - (CONV_WIDTH - 1) + k, _CONV_ROWS, stride=1), :]
                acc = acc + shifted * w[k:k + 1, :]
            y_ref[r0:r0 + _CONV_ROWS, lanes] = acc
        return carry

    lax.fori_loop(0, d // LANES, conv_lanes, 0)

    chunk = 256

    def norm_rows(i, carry):
        rows = pl.ds(pl.multiple_of(i * chunk, chunk), chunk)
        y = y_ref[rows, :]
        mu = jnp.mean(y, axis=-1, keepdims=True)
        yc = y - mu
        var = jnp.mean(yc * yc, axis=-1, keepdims=True)
        zn = yc * lax.rsqrt(var + LN_EPS) * lng_ref[...] + lnb_ref[...]
        z_ref[rows, :] = (zn * _sigmoid(zn)).astype(z_ref.dtype)
        return carry

    lax.fori_loop(0, tm // chunk, norm_rows, 0)

    o_ref[...] = (x_ref[...] + b2_ref[...]
                  + jnp.dot(z_ref[...], w2_ref[...], preferred_element_type=_F32))


def _conv_out(u, x, dw, dwb, lng, lnb, w2, b2, *, seq, tm=512):
    n, d = u.shape
    per_halo = tm // _HALO
    kern = functools.partial(_conv_out_kernel, blocks_per_seq=seq // tm)
    row = lambda i: (0, 0)
    return pl.pallas_call(
        kern,
        grid=(n // tm,),
        in_specs=[
            pl.BlockSpec((tm, d), lambda i: (i, 0)),
            pl.BlockSpec((_HALO, d), lambda i: (jnp.maximum(i * per_halo - 1, 0), 0)),
            pl.BlockSpec((tm, d), lambda i: (i, 0)),
            pl.BlockSpec((CONV_WIDTH, d), row),
            pl.BlockSpec((1, d), row),
            pl.BlockSpec((1, d), row),
            pl.BlockSpec((1, d), row),
            pl.BlockSpec((d, d), row, pipeline_mode=pl.Buffered(1)),
            pl.BlockSpec((1, d), row),
        ],
        out_specs=pl.BlockSpec((tm, d), lambda i: (i, 0)),
        out_shape=jax.ShapeDtypeStruct((n, d), _F32),
        scratch_shapes=[pltpu.VMEM((d // LANES, tm + _HALO, LANES), _F32),
                        pltpu.VMEM((tm, d), _F32),
                        pltpu.VMEM((tm, d), _BF16)],
        compiler_params=_params(1),
        name="conv_out",
    )(u, u, x, dw, dwb, lng, lnb, w2, b2)


def _ffn_kernel(x_ref, g_ref, wgu_hbm, wd_hbm, *rest, final_norm, tf):
    if final_norm:
        gf_ref, o_ref, xn_ref, wg_buf, wu_buf, wd_buf, sem = rest
    else:
        o_ref, xn_ref, wg_buf, wu_buf, wd_buf, sem = rest
    n_chunks = wd_hbm.shape[1] // tf
    i = pl.program_id(0)
    n_tiles = pl.num_programs(0)

    def chunk_copies(j, slot):
        cols = pl.ds(pl.multiple_of(j * tf, tf), tf)
        return (pltpu.make_async_copy(wgu_hbm.at[0, :, cols], wg_buf.at[slot], sem.at[0, slot]),
                pltpu.make_async_copy(wgu_hbm.at[1, :, cols], wu_buf.at[slot], sem.at[1, slot]),
                pltpu.make_async_copy(wd_hbm.at[0, cols, :], wd_buf.at[slot], sem.at[2, slot]))

    @pl.when(i == 0)
    def _():
        for copy in chunk_copies(0, 0):
            copy.start()

    _rmsnorm_rows(x_ref, [xn_ref], [g_ref])
    o_ref[...] = x_ref[...]

    def body(j, carry):
        count = i * n_chunks + j
        slot = count % 2
        is_last = jnp.logical_and(i == n_tiles - 1, j == n_chunks - 1)

        @pl.when(jnp.logical_not(is_last))
        def _():
            for copy in chunk_copies((j + 1) % n_chunks, 1 - slot):
                copy.start()

        for copy in chunk_copies(j, slot):
            copy.wait()
        xn = xn_ref[...]
        gate = jnp.dot(xn, wg_buf[slot], preferred_element_type=_F32)
        up = jnp.dot(xn, wu_buf[slot], preferred_element_type=_F32)
        act = (gate * _sigmoid(gate) * up).astype(_BF16)
        o_ref[...] += jnp.dot(act, wd_buf[slot], preferred_element_type=_F32)
        return carry

    lax.fori_loop(0, n_chunks, body, 0)

    if final_norm:
        _rmsnorm_rows(o_ref, [o_ref], [gf_ref])


def _ffn(x, g, w_gate_up, w_down, gf=None, *, tm=1024, tf=512):
    n, d = x.shape
    f = w_gate_up.shape[2]
    assert f % tf == 0
    final_norm = gf is not None
    in_specs = [
        pl.BlockSpec((tm, d), lambda i: (i, 0)),
        pl.BlockSpec((1, d), lambda i: (0, 0)),
        pl.BlockSpec(memory_space=pl.ANY),
        pl.BlockSpec(memory_space=pl.ANY),
    ]
    args = [x, g, w_gate_up, w_down]
    if final_norm:
        in_specs.append(pl.BlockSpec((1, d), lambda i: (0, 0)))
        args.append(gf)
    return pl.pallas_call(
        functools.partial(_ffn_kernel, final_norm=final_norm, tf=tf),
        grid=(n // tm,),
        in_specs=in_specs,
        out_specs=pl.BlockSpec((tm, d), lambda i: (i, 0)),
        out_shape=jax.ShapeDtypeStruct((n, d), _F32),
        scratch_shapes=[pltpu.VMEM((tm, d), _BF16),
                        pltpu.VMEM((2, d, tf), _BF16),
                        pltpu.VMEM((2, d, tf), _BF16),
                        pltpu.VMEM((2, tf, d), _BF16),
                        pltpu.SemaphoreType.DMA((3, 2))],
        compiler_params=_params(1),
        name="ffn_final" if final_norm else "ffn",
    )(*args)


def _kvq_kernel(x_ref, w_ref, gate_ref, up_ref, down_ref, o_ref, wgu_ref, wdn_ref, xn_ref,
                *, q_scale, blocks_per_proj, third):
    j = pl.program_id(1)
    step = pl.program_id(0) * pl.num_programs(1) + j

    @pl.when(j == 0)
    def _():
        _rmsnorm_rows(x_ref, [xn_ref], [None])

    @pl.when(j < 2 * blocks_per_proj)
    def _():
        o_ref[...] = jnp.dot(xn_ref[...], w_ref[...], preferred_element_type=_F32)

    @pl.when(j >= 2 * blocks_per_proj)
    def _():
        o_ref[...] = jnp.dot(xn_ref[...], w_ref[...], preferred_element_type=_F32) * q_scale

    @pl.when(step < third)
    def _():
        wgu_ref[...] = gate_ref[...].astype(wgu_ref.dtype)

    @pl.when(jnp.logical_and(step >= third, step < 2 * third))
    def _():
        wgu_ref[...] = up_ref[...].astype(wgu_ref.dtype)

    @pl.when(step >= 2 * third)
    def _():
        wdn_ref[...] = down_ref[...].astype(wdn_ref.dtype)


def _kvq(x, w_kvq, ffn_gate, ffn_up, ffn_down, layer, *, q_scale, tm=1024, tn=1024):
    n, d = x.shape
    f = ffn_gate.shape[2]
    per = d // tn
    steps_j = 3 * per
    third = (n // tm) * steps_j // 3
    gate_rows, down_rows = d // third, f // third
    assert third * 3 == (n // tm) * steps_j and gate_rows * third == d and down_rows * third == f
    assert gate_rows % 16 == 0 and down_rows % 16 == 0

    def block(i, j, k):
        return jnp.clip(i * steps_j + j - k * third, 0, third - 1)

    def gu_index(i, j):
        step = i * steps_j + j
        return jnp.minimum(step // third, 1), jnp.where(step < 2 * third, step % third, third - 1), 0

    return pl.pallas_call(
        functools.partial(_kvq_kernel, q_scale=q_scale, blocks_per_proj=per, third=third),
        grid=(n // tm, steps_j),
        in_specs=[
            pl.BlockSpec((tm, d), lambda i, j: (i, 0)),
            pl.BlockSpec((None, d, tn), lambda i, j: (j // per, 0, j % per)),
            pl.BlockSpec((None, gate_rows, f), lambda i, j: (layer, block(i, j, 0), 0)),
            pl.BlockSpec((None, gate_rows, f), lambda i, j: (layer, block(i, j, 1), 0)),
            pl.BlockSpec((None, down_rows, d), lambda i, j: (layer, block(i, j, 2), 0)),
        ],
        out_specs=[
            pl.BlockSpec((None, tm, tn), lambda i, j: (j // per, i, j % per)),
            pl.BlockSpec((None, gate_rows, f), gu_index),
            pl.BlockSpec((None, down_rows, d), lambda i, j: (0, block(i, j, 2), 0)),
        ],
        out_shape=[jax.ShapeDtypeStruct((3, n, d), _F32),
                   jax.ShapeDtypeStruct((2, d, f), _BF16),
                   jax.ShapeDtypeStruct((1, f, d), _BF16)],
        scratch_shapes=[pltpu.VMEM((tm, d), _BF16)],
        compiler_params=_params(2),
        name="kvq",
    )(x, w_kvq, ffn_gate, ffn_up, ffn_down)


def _step_tables():
    p = np.arange(Q_BLOCK)[:, None]
    c = np.arange(2 * Q_BLOCK)[None, :]
    pair = []
    for shift in (0, Q_BLOCK):
        j = shift + p - c
        valid = (j >= 0) & (j <= WINDOW_STEPS)
        pair.append(np.where(valid, j.astype(np.float32), np.float32(MASKED)))
    return np.stack(pair).astype(np.float32)


def _attention_kernel(slopes_ref, tab_ref, q_ref, k_ref, v_ref, o_ref,
                      bias_ref, qs_ref, ks_ref, vs_ref, t4_ref, s_ref, e_ref, rl_ref, ob_ref, lse_ref):
    seq = q_ref.shape[0]
    blocks_per_branch = seq // Q_BLOCK
    n_br = len(BRANCH_DILATIONS)
    n_groups = n_br * blocks_per_branch // GROUP
    assert blocks_per_branch % GROUP == 0 and n_groups % 2 == 0 and n_groups >= 4
    h = pl.program_id(1)
    neg_slope2 = -slopes_ref[h] * LOG2E
    for b, dil in enumerate(BRANCH_DILATIONS):
        for t in range(2):
            bias_ref[b, t] = tab_ref[t] * (dil * neg_slope2)

    ratio = DILATION_RATIO
    chunk = seq // max(BRANCH_DILATIONS)
    quarter = seq // ratio

    def finer(c):
        return pl.ds((c // ratio) + ratio * (c % ratio) * chunk, chunk, stride=ratio)

    def finer16(c):
        return pl.ds((c % ratio) * quarter + c // ratio, chunk, stride=ratio)

    srcs = (q_ref, k_ref, v_ref)
    dsts = (qs_ref, ks_ref, vs_ref)

    def regroup4(c, carry):
        dst = pl.ds(pl.multiple_of(c * chunk, chunk), chunk)
        for a in range(3):
            dsts[a][0, dst, :] = srcs[a][dst, :].astype(_BF16)
            by4 = srcs[a][finer(c), :]
            t4_ref[a, dst, :] = by4
            dsts[a][1, dst, :] = by4.astype(_BF16)
        return carry

    lax.fori_loop(0, seq // chunk, regroup4, 0, unroll=2)

    def regroup16(c, carry):
        dst = pl.ds(pl.multiple_of(c * chunk, chunk), chunk)
        for a in range(3):
            dsts[a][2, dst, :] = t4_ref[a, finer16(c), :].astype(_BF16)
        return carry

    lax.fori_loop(0, seq // chunk, regroup16, 0, unroll=2)

    def locate(g, u):
        t = g * GROUP + u
        b = t // blocks_per_branch
        jb = t % blocks_per_branch
        per_class = lax.shift_right_logical(blocks_per_branch, 2 * b)
        is_first = (jb & (per_class - 1)) == 0
        q0 = pl.multiple_of(jb * Q_BLOCK, Q_BLOCK)
        k0 = pl.multiple_of(jnp.where(is_first, q0, q0 - Q_BLOCK), Q_BLOCK)
        return b, q0, k0, jnp.where(is_first, 0, 1)

    def scores(g, slot):
        for u in range(GROUP):
            b, q0, k0, _ = locate(g, u)
            q = qs_ref[b, pl.ds(q0, Q_BLOCK), :]
            k = ks_ref[b, pl.ds(k0, 2 * Q_BLOCK), :]
            s_ref[slot, u] = lax.dot_general(q, k, (((1,), (1,)), ((), ())),
                                             preferred_element_type=_F32)

    def softmax(g, slot):
        for u in range(GROUP):
            b, q0, _, tab = locate(g, u)
            s = s_ref[slot, u] + bias_ref[b, tab]
            m = jnp.max(s, axis=-1, keepdims=True)
            e = jnp.exp2(s - m)
            l = jnp.sum(e, axis=-1, keepdims=True)
            e_ref[slot, u] = e.astype(_BF16)
            rl_ref[slot, u] = jnp.broadcast_to(1.0 / l, (Q_BLOCK, LANES))
            lse_ref[b, pl.ds(q0, Q_BLOCK), :] = jnp.broadcast_to(m + jnp.log2(l), (Q_BLOCK, LANES))

    def values(g, slot):
        for u in range(GROUP):
            b, q0, k0, _ = locate(g, u)
            v = vs_ref[b, pl.ds(k0, 2 * Q_BLOCK), :]
            acc = jnp.dot(e_ref[slot, u], v, preferred_element_type=_F32)
            ob_ref[b, pl.ds(q0, Q_BLOCK), :] = acc * rl_ref[slot, u]

    zero = jnp.int32(0)
    scores(zero, 0)
    softmax(zero, 0)
    scores(zero + 1, 1)

    def trip(i, carry):
        g = 2 * i
        values(g - 2, 0)
        softmax(g - 1, 1)
        scores(g, 0)
        values(g - 1, 1)
        softmax(g, 0)
        scores(g + 1, 1)
        return carry

    lax.fori_loop(1, n_groups // 2, trip, 0)
    values(zero + (n_groups - 2), 0)
    softmax(zero + (n_groups - 1), 1)
    values(zero + (n_groups - 1), 1)

    def coarsen(c, carry):
        src = pl.ds(pl.multiple_of(c * chunk, chunk), chunk)
        t4_ref[0, finer16(c), :] = ob_ref[2, src, :]
        t4_ref[1, finer16(c), :] = lse_ref[2, src, :]
        return carry

    lax.fori_loop(0, seq // chunk, coarsen, 0, unroll=2)

    rows_per = 1024

    def merge(c, carry):
        by4 = pl.ds(pl.multiple_of(c * rows_per, rows_per), rows_per)
        per_class = quarter // rows_per
        natural = pl.ds((c // per_class) + ratio * (c % per_class) * rows_per, rows_per, stride=ratio)
        ls = [lse_ref[0, natural, :], lse_ref[1, by4, :], t4_ref[1, by4, :]]
        os = [ob_ref[0, natural, :], ob_ref[1, by4, :], t4_ref[0, by4, :]]
        l_max = functools.reduce(jnp.maximum, ls)
        num = 0.0
        den = 0.0
        for b in range(n_br):
            w = jnp.exp2(ls[b] - l_max)
            num = num + w * os[b]
            den = den + w
        o_ref[natural, :] = num / den
        return carry

    lax.fori_loop(0, seq // rows_per, merge, 0)


def _attention(kvq, slopes, tables, *, batch, seq):
    _, n, d = kvq.shape
    e = d // N_HEADS
    n_br = len(BRANCH_DILATIONS)
    kvq4 = kvq.reshape(3, batch, seq, d)

    def qkv_spec(which):
        return pl.BlockSpec((None, None, seq, e), lambda b, h: (which, b, 0, h))

    return pl.pallas_call(
        _attention_kernel,
        grid=(batch, N_HEADS),
        in_specs=[
            pl.BlockSpec(memory_space=pltpu.SMEM),
            pl.BlockSpec(tables.shape, lambda b, h: (0, 0, 0)),
            qkv_spec(2), qkv_spec(0), qkv_spec(1),
        ],
        out_specs=pl.BlockSpec((None, seq, e), lambda b, h: (b, 0, h)),
        out_shape=jax.ShapeDtypeStruct((batch, seq, d), _F32),
        scratch_shapes=[
            pltpu.VMEM((n_br, 2, Q_BLOCK, 2 * Q_BLOCK), _F32),
            pltpu.VMEM((n_br, seq, e), _BF16),
            pltpu.VMEM((n_br, seq, e), _BF16),
            pltpu.VMEM((n_br, seq, e), _BF16),
            pltpu.VMEM((3, seq, e), _F32),
            pltpu.VMEM((2, GROUP, Q_BLOCK, 2 * Q_BLOCK), _F32),
            pltpu.VMEM((2, GROUP, Q_BLOCK, 2 * Q_BLOCK), _BF16),
            pltpu.VMEM((2, GROUP, Q_BLOCK, LANES), _F32),
            pltpu.VMEM((n_br, seq, e), _F32),
            pltpu.VMEM((n_br, seq, LANES), _F32),
        ],
        compiler_params=_params(2),
        name="dilated_attention",
    )(slopes, tables, kvq4, kvq4, kvq4)


def _out_proj_kernel(a_ref, x_ref, w_ref, o_ref):
    o_ref[...] = x_ref[...] + jnp.dot(a_ref[...].astype(_BF16), w_ref[...],
                                      preferred_element_type=_F32)


def _out_proj(att, x, w, *, tm=512):
    n, d = x.shape
    return pl.pallas_call(
        _out_proj_kernel,
        grid=(n // tm,),
        in_specs=[
            pl.BlockSpec((tm, d), lambda i: (i, 0)),
            pl.BlockSpec((tm, d), lambda i: (i, 0)),
            pl.BlockSpec((d, d), lambda i: (0, 0)),
        ],
        out_specs=pl.BlockSpec((tm, d), lambda i: (i, 0)),
        out_shape=jax.ShapeDtypeStruct((n, d), _F32),
        compiler_params=_params(1),
        name="out_proj",
    )(att, x, w)


def kernel(x, a_norm_g, conv_w1, conv_b1, conv_dw, conv_dw_b, conv_ln_g, conv_ln_b, conv_w2, conv_b2,
           kv_norm_g, w_k, w_v, b_norm_g, w_q, w_o, ffn_norm_g, ffn_w_gate, ffn_w_up, ffn_w_down,
           final_norm_g):
    batch, seq, d = x.shape
    n_a = a_norm_g.shape[0]
    n_b = b_norm_g.shape[0]
    depth = n_a + n_b
    assert d % N_HEADS == 0 and d // N_HEADS == LANES
    assert BRANCH_DILATIONS == (1, 4, 16) and seq % (max(BRANCH_DILATIONS) * Q_BLOCK * 2) == 0

    row = lambda v: v.reshape(1, -1)
    h = x.reshape(batch * seq, d)
    head_dim = d // N_HEADS
    slopes = jnp.exp2(-8.0 * jnp.arange(1, N_HEADS + 1, dtype=_F32) / N_HEADS)
    tables = jnp.asarray(_step_tables())
    kvq = None

    for layer in range(depth):
        ffn_weights = None
        if layer < n_a:
            a = layer
            w1 = _cast_stack([(conv_w1, a)])[0]
            w2 = _cast_stack([(conv_w2, a)])[0]
            u = _glu_in(h, row(a_norm_g[a]), w1, row(conv_b1[a]))
            h = _conv_out(u, h, conv_dw[a], row(conv_dw_b[a]), row(conv_ln_g[a]), row(conv_ln_b[a]),
                          w2, row(conv_b2[a]), seq=seq)
        else:
            i = layer - n_a
            assert n_b == 1, "one attention layer supported"
            w_kvq = _cast_stack([(w_k, None), (w_v, None), (w_q, i)],
                                row_scales=[kv_norm_g, kv_norm_g, b_norm_g[i]])
            kvq, *ffn_weights = _kvq(h, w_kvq, ffn_w_gate, ffn_w_up, ffn_w_down, layer,
                                     q_scale=head_dim ** -0.5 * LOG2E)
            att = _attention(kvq, slopes, tables, batch=batch, seq=seq)
            h = _out_proj(att.reshape(batch * seq, d), h, _cast_stack([(w_o, i)])[0])
        last = layer == depth - 1
        if ffn_weights is None:
            ffn_weights = (_cast_stack([(ffn_w_gate, layer), (ffn_w_up, layer)]),
                           _cast_stack([(ffn_w_down, layer)]))
        h = _ffn(h, row(ffn_norm_g[layer]), *ffn_weights, row(final_norm_g) if last else None)
    return h.reshape(batch, seq, d)
```

```python
import functools
import math

import numpy as np
import jax
import jax.numpy as jnp
from jax import lax
from jax.experimental import pallas as pl
from jax.experimental.pallas import tpu as pltpu

RMS_EPS = 1e-6
LN_EPS = 1e-5
N_HEADS = 16
CONV_WIDTH = 31
BRANCH_DILATIONS = (1, 4, 16)
DILATION_RATIO = 4
WINDOW_STEPS = 128
Q_BLOCK = 128
GROUP = 4
MASKED = 1e30
LOG2E = math.log2(math.e)
LANES = 128

_VMEM_LIMIT = 56 * 1024 * 1024
_BF16 = jnp.bfloat16
_F32 = jnp.float32


def _params(n_axes):
    return pltpu.CompilerParams(dimension_semantics=("arbitrary",) * n_axes,
                                vmem_limit_bytes=_VMEM_LIMIT)


def _sigmoid(x):
    return 1.0 / (1.0 + jnp.exp(-x))


def _rmsnorm_rows(x_ref, out_refs, g_refs, chunk=256):
    n = x_ref.shape[0] // chunk

    def body(i, carry):
        r = pl.multiple_of(i * chunk, chunk)
        x = x_ref[pl.ds(r, chunk), :]
        xhat = x * lax.rsqrt(jnp.mean(x * x, axis=-1, keepdims=True) + RMS_EPS)
        for o_ref, g_ref in zip(out_refs, g_refs):
            y = xhat if g_ref is None else xhat * g_ref[...]
            o_ref[pl.ds(r, chunk), :] = y.astype(o_ref.dtype)
        return carry

    lax.fori_loop(0, n, body, 0)


def _cast_kernel(*refs, n_src, scaled):
    src_refs, scale_refs, o_ref = refs[:n_src], refs[n_src:-1], refs[-1]
    which = pl.program_id(0)
    for k, src_ref in enumerate(src_refs):
        @pl.when(which == k)
        def _(k=k, src_ref=src_ref):
            w = src_ref[...]
            if scaled:
                w = w * scale_refs[k][...]
            o_ref[...] = w.astype(o_ref.dtype)


def _cast_stack(sources, row_scales=None):
    rows, cols = sources[0][0].shape[-2:]
    block_elems = (1 << 21) if len(sources) <= 2 else (1 << 20)
    rb = 1 << ((block_elems // cols).bit_length() - 1)
    while rows % rb:
        rb //= 2
    assert rb >= 16
    nb = rows // rb

    def in_spec(k, layer, width):
        def block(which, b):
            return jnp.where(which == k, b, jnp.where(which < k, 0, nb - 1))
        if layer is None:
            return pl.BlockSpec((rb, width), lambda which, b: (block(which, b), 0))
        return pl.BlockSpec((None, rb, width), lambda which, b: (layer, block(which, b), 0))

    in_specs = [in_spec(k, layer, cols) for k, (_, layer) in enumerate(sources)]
    args = [a for a, _ in sources]
    if row_scales is not None:
        in_specs += [in_spec(k, None, 1) for k in range(len(sources))]
        args += [v.reshape(rows, 1) for v in row_scales]
    return pl.pallas_call(
        functools.partial(_cast_kernel, n_src=len(sources), scaled=row_scales is not None),
        grid=(len(sources), nb),
        in_specs=in_specs,
        out_specs=pl.BlockSpec((None, rb, cols), lambda which, b: (which, b, 0)),
        out_shape=jax.ShapeDtypeStruct((len(sources), rows, cols), _BF16),
        compiler_params=_params(2),
        name="cast_bf16",
    )(*args)


def _glu_in_kernel(x_ref, g_ref, wa_ref, wg_ref, ba_ref, bg_ref, u_ref, xn_ref):
    @pl.when(pl.program_id(1) == 0)
    def _():
        _rmsnorm_rows(x_ref, [xn_ref], [g_ref])

    xn = xn_ref[...]
    a = jnp.dot(xn, wa_ref[...], preferred_element_type=_F32) + ba_ref[...]
    gate = jnp.dot(xn, wg_ref[...], preferred_element_type=_F32) + bg_ref[...]
    u_ref[...] = a * _sigmoid(gate)


def _glu_in(x, g, w1, b1, *, tm=1024, tn=1024):
    n, d = x.shape
    nj = d // tn
    return pl.pallas_call(
        _glu_in_kernel,
        grid=(n // tm, nj),
        in_specs=[
            pl.BlockSpec((tm, d), lambda i, j: (i, 0)),
            pl.BlockSpec((1, d), lambda i, j: (0, 0)),
            pl.BlockSpec((d, tn), lambda i, j: (0, j)),
            pl.BlockSpec((d, tn), lambda i, j: (0, j + nj)),
            pl.BlockSpec((1, tn), lambda i, j: (0, j)),
            pl.BlockSpec((1, tn), lambda i, j: (0, j + nj)),
        ],
        out_specs=pl.BlockSpec((tm, tn), lambda i, j: (i, j)),
        out_shape=jax.ShapeDtypeStruct((n, d), _F32),
        scratch_shapes=[pltpu.VMEM((tm, d), _BF16)],
        compiler_params=_params(2),
        name="glu_in",
    )(x, g, w1, w1, b1, b1)


_HALO = 32
_CONV_ROWS = 256


def _conv_out_kernel(u_ref, halo_ref, x_ref, dw_ref, dwb_ref, lng_ref, lnb_ref, w2_ref, b2_ref,
                     o_ref, buf_ref, y_ref, z_ref, *, blocks_per_seq):
    tm, d = u_ref.shape
    first = (pl.program_id(0) % blocks_per_seq) == 0
    for c in range(d // LANES):
        lanes = slice(c * LANES, (c + 1) * LANES)
        buf_ref[c, 0:_HALO, :] = jnp.where(first, 0.0, halo_ref[:, lanes])
        buf_ref[c, _HALO:, :] = u_ref[:, lanes]

    def conv_lanes(c, carry):
        lanes = pl.ds(pl.multiple_of(c * LANES, LANES), LANES)
        w = dw_ref[:, lanes]
        for r0 in range(0, tm, _CONV_ROWS):
            acc = jnp.broadcast_to(dwb_ref[:, lanes], (_CONV_ROWS, LANES))
            for k in range(CONV_WIDTH):
                shifted = buf_ref[c, pl.ds(r0 + _HALO - (CONV_WIDTH - 1) + k, _CONV_ROWS, stride=1), :]
                acc = acc + shifted * w[k:k + 1, :]
            y_ref[r0:r0 + _CONV_ROWS, lanes] = acc
        return carry

    lax.fori_loop(0, d // LANES, conv_lanes, 0)

    chunk = 256

    def norm_rows(i, carry):
        rows = pl.ds(pl.multiple_of(i * chunk, chunk), chunk)
        y = y_ref[rows, :]
        mu = jnp.mean(y, axis=-1, keepdims=True)
        yc = y - mu
        var = jnp.mean(yc * yc, axis=-1, keepdims=True)
        zn = yc * lax.rsqrt(var + LN_EPS) * lng_ref[...] + lnb_ref[...]
        z_ref[rows, :] = (zn * _sigmoid(zn)).astype(z_ref.dtype)
        return carry

    lax.fori_loop(0, tm // chunk, norm_rows, 0)

    o_ref[...] = (x_ref[...] + b2_ref[...]
                  + jnp.dot(z_ref[...], w2_ref[...], preferred_element_type=_F32))


def _conv_out(u, x, dw, dwb, lng, lnb, w2, b2, *, seq, tm=512):
    n, d = u.shape
    per_halo = tm // _HALO
    kern = functools.partial(_conv_out_kernel, blocks_per_seq=seq // tm)
    row = lambda i: (0, 0)
    return pl.pallas_call(
        kern,
        grid=(n // tm,),
        in_specs=[
            pl.BlockSpec((tm, d), lambda i: (i, 0)),
            pl.BlockSpec((_HALO, d), lambda i: (jnp.maximum(i * per_halo - 1, 0), 0)),
            pl.BlockSpec((tm, d), lambda i: (i, 0)),
            pl.BlockSpec((CONV_WIDTH, d), row),
            pl.BlockSpec((1, d), row),
            pl.BlockSpec((1, d), row),
            pl.BlockSpec((1, d), row),
            pl.BlockSpec((d, d), row, pipeline_mode=pl.Buffered(1)),
            pl.BlockSpec((1, d), row),
        ],
        out_specs=pl.BlockSpec((tm, d), lambda i: (i, 0)),
        out_shape=jax.ShapeDtypeStruct((n, d), _F32),
        scratch_shapes=[pltpu.VMEM((d // LANES, tm + _HALO, LANES), _F32),
                        pltpu.VMEM((tm, d), _F32),
                        pltpu.VMEM((tm, d), _BF16)],
        compiler_params=_params(1),
        name="conv_out",
    )(u, u, x, dw, dwb, lng, lnb, w2, b2)


def _ffn_kernel(x_ref, g_ref, wgu_hbm, wd_hbm, *rest, final_norm, tf):
    if final_norm:
        gf_ref, o_ref, xn_ref, wg_buf, wu_buf, wd_buf, sem = rest
    else:
        o_ref, xn_ref, wg_buf, wu_buf, wd_buf, sem = rest
    n_chunks = wd_hbm.shape[1] // tf
    i = pl.program_id(0)
    n_tiles = pl.num_programs(0)

    def chunk_copies(j, slot):
        cols = pl.ds(pl.multiple_of(j * tf, tf), tf)
        return (pltpu.make_async_copy(wgu_hbm.at[0, :, cols], wg_buf.at[slot], sem.at[0, slot]),
                pltpu.make_async_copy(wgu_hbm.at[1, :, cols], wu_buf.at[slot], sem.at[1, slot]),
                pltpu.make_async_copy(wd_hbm.at[0, cols, :], wd_buf.at[slot], sem.at[2, slot]))

    @pl.when(i == 0)
    def _():
        for copy in chunk_copies(0, 0):
            copy.start()

    _rmsnorm_rows(x_ref, [xn_ref], [g_ref])
    o_ref[...] = x_ref[...]

    def body(j, carry):
        count = i * n_chunks + j
        slot = count % 2
        is_last = jnp.logical_and(i == n_tiles - 1, j == n_chunks - 1)

        @pl.when(jnp.logical_not(is_last))
        def _():
            for copy in chunk_copies((j + 1) % n_chunks, 1 - slot):
                copy.start()

        for copy in chunk_copies(j, slot):
            copy.wait()
        xn = xn_ref[...]
        gate = jnp.dot(xn, wg_buf[slot], preferred_element_type=_F32)
        up = jnp.dot(xn, wu_buf[slot], preferred_element_type=_F32)
        act = (gate * _sigmoid(gate) * up).astype(_BF16)
        o_ref[...] += jnp.dot(act, wd_buf[slot], preferred_element_type=_F32)
        return carry

    lax.fori_loop(0, n_chunks, body, 0)

    if final_norm:
        _rmsnorm_rows(o_ref, [o_ref], [gf_ref])


def _ffn(x, g, w_gate_up, w_down, gf=None, *, tm=1024, tf=512):
    n, d = x.shape
    f = w_gate_up.shape[2]
    assert f % tf == 0
    final_norm = gf is not None
    in_specs = [
        pl.BlockSpec((tm, d), lambda i: (i, 0)),
        pl.BlockSpec((1, d), lambda i: (0, 0)),
        pl.BlockSpec(memory_space=pl.ANY),
        pl.BlockSpec(memory_space=pl.ANY),
    ]
    args = [x, g, w_gate_up, w_down]
    if final_norm:
        in_specs.append(pl.BlockSpec((1, d), lambda i: (0, 0)))
        args.append(gf)
    return pl.pallas_call(
        functools.partial(_ffn_kernel, final_norm=final_norm, tf=tf),
        grid=(n // tm,),
        in_specs=in_specs,
        out_specs=pl.BlockSpec((tm, d), lambda i: (i, 0)),
        out_shape=jax.ShapeDtypeStruct((n, d), _F32),
        scratch_shapes=[pltpu.VMEM((tm, d), _BF16),
                        pltpu.VMEM((2, d, tf), _BF16),
                        pltpu.VMEM((2, d, tf), _BF16),
                        pltpu.VMEM((2, tf, d), _BF16),
                        pltpu.SemaphoreType.DMA((3, 2))],
        compiler_params=_params(1),
        name="ffn_final" if final_norm else "ffn",
    )(*args)


def _kvq_kernel(x_ref, w_ref, gate_ref, up_ref, down_ref, o_ref, wgu_ref, wdn_ref, xn_ref,
                *, q_scale, blocks_per_proj, third):
    j = pl.program_id(1)
    step = pl.program_id(0) * pl.num_programs(1) + j

    @pl.when(j == 0)
    def _():
        _rmsnorm_rows(x_ref, [xn_ref], [None])

    @pl.when(j < 2 * blocks_per_proj)
    def _():
        o_ref[...] = jnp.dot(xn_ref[...], w_ref[...], preferred_element_type=_F32)

    @pl.when(j >= 2 * blocks_per_proj)
    def _():
        o_ref[...] = jnp.dot(xn_ref[...], w_ref[...], preferred_element_type=_F32) * q_scale

    @pl.when(step < third)
    def _():
        wgu_ref[...] = gate_ref[...].astype(wgu_ref.dtype)

    @pl.when(jnp.logical_and(step >= third, step < 2 * third))
    def _():
        wgu_ref[...] = up_ref[...].astype(wgu_ref.dtype)

    @pl.when(step >= 2 * third)
    def _():
        wdn_ref[...] = down_ref[...].astype(wdn_ref.dtype)


def _kvq(x, w_kvq, ffn_gate, ffn_up, ffn_down, layer, *, q_scale, tm=1024, tn=1024):
    n, d = x.shape
    f = ffn_gate.shape[2]
    per = d // tn
    steps_j = 3 * per
    third = (n // tm) * steps_j // 3
    gate_rows, down_rows = d // third, f // third
    assert third * 3 == (n // tm) * steps_j and gate_rows * third == d and down_rows * third == f
    assert gate_rows % 16 == 0 and down_rows % 16 == 0

    def block(i, j, k):
        return jnp.clip(i * steps_j + j - k * third, 0, third - 1)

    def gu_index(i, j):
        step = i * steps_j + j
        return jnp.minimum(step // third, 1), jnp.where(step < 2 * third, step % third, third - 1), 0

    return pl.pallas_call(
        functools.partial(_kvq_kernel, q_scale=q_scale, blocks_per_proj=per, third=third),
        grid=(n // tm, steps_j),
        in_specs=[
            pl.BlockSpec((tm, d), lambda i, j: (i, 0)),
            pl.BlockSpec((None, d, tn), lambda i, j: (j // per, 0, j % per)),
            pl.BlockSpec((None, gate_rows, f), lambda i, j: (layer, block(i, j, 0), 0)),
            pl.BlockSpec((None, gate_rows, f), lambda i, j: (layer, block(i, j, 1), 0)),
            pl.BlockSpec((None, down_rows, d), lambda i, j: (layer, block(i, j, 2), 0)),
        ],
        out_specs=[
            pl.BlockSpec((None, tm, tn), lambda i, j: (j // per, i, j % per)),
            pl.BlockSpec((None, gate_rows, f), gu_index),
            pl.BlockSpec((None, down_rows, d), lambda i, j: (0, block(i, j, 2), 0)),
        ],
        out_shape=[jax.ShapeDtypeStruct((3, n, d), _F32),
                   jax.ShapeDtypeStruct((2, d, f), _BF16),
                   jax.ShapeDtypeStruct((1, f, d), _BF16)],
        scratch_shapes=[pltpu.VMEM((tm, d), _BF16)],
        compiler_params=_params(2),
        name="kvq",
    )(x, w_kvq, ffn_gate, ffn_up, ffn_down)


def _step_tables():
    p = np.arange(Q_BLOCK)[:, None]
    c = np.arange(2 * Q_BLOCK)[None, :]
    pair = []
    for shift in (0, Q_BLOCK):
        j = shift + p - c
        valid = (j >= 0) & (j <= WINDOW_STEPS)
        pair.append(np.where(valid, j.astype(np.float32), np.float32(MASKED)))
    return np.stack(pair).astype(np.float32)


def _attention_kernel(slopes_ref, tab_ref, q_ref, k_ref, v_ref, o_ref,
                      bias_ref, qs_ref, ks_ref, vs_ref, t4_ref, s_ref, e_ref, rl_ref, ob_ref, lse_ref):
    seq = q_ref.shape[0]
    blocks_per_branch = seq // Q_BLOCK
    n_br = len(BRANCH_DILATIONS)
    n_groups = n_br * blocks_per_branch // GROUP
    assert blocks_per_branch % GROUP == 0 and n_groups % 2 == 0 and n_groups >= 4
    h = pl.program_id(1)
    neg_slope2 = -slopes_ref[h] * LOG2E
    for b, dil in enumerate(BRANCH_DILATIONS):
        for t in range(2):
            bias_ref[b, t] = tab_ref[t] * (dil * neg_slope2)

    ratio = DILATION_RATIO
    chunk = seq // max(BRANCH_DILATIONS)
    quarter = seq // ratio

    def finer(c):
        return pl.ds((c // ratio) + ratio * (c % ratio) * chunk, chunk, stride=ratio)

    def finer16(c):
        return pl.ds((c % ratio) * quarter + c // ratio, chunk, stride=ratio)

    srcs = (q_ref, k_ref, v_ref)
    dsts = (qs_ref, ks_ref, vs_ref)

    def regroup4(c, carry):
        dst = pl.ds(pl.multiple_of(c * chunk, chunk), chunk)
        for a in range(3):
            dsts[a][0, dst, :] = srcs[a][dst, :].astype(_BF16)
            by4 = srcs[a][finer(c), :]
            t4_ref[a, dst, :] = by4
            dsts[a][1, dst, :] = by4.astype(_BF16)
        return carry

    lax.fori_loop(0, seq // chunk, regroup4, 0, unroll=2)

    def regroup16(c, carry):
        dst = pl.ds(pl.multiple_of(c * chunk, chunk), chunk)
        for a in range(3):
            dsts[a][2, dst, :] = t4_ref[a, finer16(c), :].astype(_BF16)
        return carry

    lax.fori_loop(0, seq // chunk, regroup16, 0, unroll=2)

    def locate(g, u):
        t = g * GROUP + u
        b = t // blocks_per_branch
        jb = t % blocks_per_branch
        per_class = lax.shift_right_logical(blocks_per_branch, 2 * b)
        is_first = (jb & (per_class - 1)) == 0
        q0 = pl.multiple_of(jb * Q_BLOCK, Q_BLOCK)
        k0 = pl.multiple_of(jnp.where(is_first, q0, q0 - Q_BLOCK), Q_BLOCK)
        return b, q0, k0, jnp.where(is_first, 0, 1)

    def scores(g, slot):
        for u in range(GROUP):
            b, q0, k0, _ = locate(g, u)
            q = qs_ref[b, pl.ds(q0, Q_BLOCK), :]
            k = ks_ref[b, pl.ds(k0, 2 * Q_BLOCK), :]
            s_ref[slot, u] = lax.dot_general(q, k, (((1,), (1,)), ((), ())),
                                             preferred_element_type=_F32)

    def softmax(g, slot):
        for u in range(GROUP):
            b, q0, _, tab = locate(g, u)
            s = s_ref[slot, u] + bias_ref[b, tab]
            m = jnp.max(s, axis=-1, keepdims=True)
            e = jnp.exp2(s - m)
            l = jnp.sum(e, axis=-1, keepdims=True)
            e_ref[slot, u] = e.astype(_BF16)
            rl_ref[slot, u] = jnp.broadcast_to(1.0 / l, (Q_BLOCK, LANES))
            lse_ref[b, pl.ds(q0, Q_BLOCK), :] = jnp.broadcast_to(m + jnp.log2(l), (Q_BLOCK, LANES))

    def values(g, slot):
        for u in range(GROUP):
            b, q0, k0, _ = locate(g, u)
            v = vs_ref[b, pl.ds(k0, 2 * Q_BLOCK), :]
            acc = jnp.dot(e_ref[slot, u], v, preferred_element_type=_F32)
            ob_ref[b, pl.ds(q0, Q_BLOCK), :] = acc * rl_ref[slot, u]

    zero = jnp.int32(0)
    scores(zero, 0)
    softmax(zero, 0)
    scores(zero + 1, 1)

    def trip(i, carry):
        g = 2 * i
        values(g - 2, 0)
        softmax(g - 1, 1)
        scores(g, 0)
        values(g - 1, 1)
        softmax(g, 0)
        scores(g + 1, 1)
        return carry

    lax.fori_loop(1, n_groups // 2, trip, 0, unroll=True)
    values(zero + (n_groups - 2), 0)
    softmax(zero + (n_groups - 1), 1)
    values(zero + (n_groups - 1), 1)

    def coarsen(c, carry):
        src = pl.ds(pl.multiple_of(c * chunk, chunk), chunk)
        t4_ref[0, finer16(c), :] = ob_ref[2, src, :]
        t4_ref[1, finer16(c), :] = lse_ref[2, src, :]
        return carry

    lax.fori_loop(0, seq // chunk, coarsen, 0, unroll=2)

    rows_per = 1024

    def merge(c, carry):
        by4 = pl.ds(pl.multiple_of(c * rows_per, rows_per), rows_per)
        per_class = quarter // rows_per
        natural = pl.ds((c // per_class) + ratio * (c % per_class) * rows_per, rows_per, stride=ratio)
        ls = [lse_ref[0, natural, :], lse_ref[1, by4, :], t4_ref[1, by4, :]]
        os = [ob_ref[0, natural, :], ob_ref[1, by4, :], t4_ref[0, by4, :]]
        l_max = functools.reduce(jnp.maximum, ls)
        num = 0.0
        den = 0.0
        for b in range(n_br):
            w = jnp.exp2(ls[b] - l_max)
            num = num + w * os[b]
            den = den + w
        o_ref[natural, :] = num / den
        return carry

    lax.fori_loop(0, seq // rows_per, merge, 0)


def _attention(kvq, slopes, tables, *, batch, seq):
    _, n, d = kvq.shape
    e = d // N_HEADS
    n_br = len(BRANCH_DILATIONS)
    kvq4 = kvq.reshape(3, batch, seq, d)

    def qkv_spec(which):
        return pl.BlockSpec((None, None, seq, e), lambda b, h: (which, b, 0, h))

    return pl.pallas_call(
        _attention_kernel,
        grid=(batch, N_HEADS),
        in_specs=[
            pl.BlockSpec(memory_space=pltpu.SMEM),
            pl.BlockSpec(tables.shape, lambda b, h: (0, 0, 0)),
            qkv_spec(2), qkv_spec(0), qkv_spec(1),
        ],
        out_specs=pl.BlockSpec((None, seq, e), lambda b, h: (b, 0, h)),
        out_shape=jax.ShapeDtypeStruct((batch, seq, d), _F32),
        scratch_shapes=[
            pltpu.VMEM((n_br, 2, Q_BLOCK, 2 * Q_BLOCK), _F32),
            pltpu.VMEM((n_br, seq, e), _BF16),
            pltpu.VMEM((n_br, seq, e), _BF16),
            pltpu.VMEM((n_br, seq, e), _BF16),
            pltpu.VMEM((3, seq, e), _F32),
            pltpu.VMEM((2, GROUP, Q_BLOCK, 2 * Q_BLOCK), _F32),
            pltpu.VMEM((2, GROUP, Q_BLOCK, 2 * Q_BLOCK), _BF16),
            pltpu.VMEM((2, GROUP, Q_BLOCK, LANES), _F32),
            pltpu.VMEM((n_br, seq, e), _F32),
            pltpu.VMEM((n_br, seq, LANES), _F32),
        ],
        compiler_params=_params(2),
        name="dilated_attention",
    )(slopes, tables, kvq4, kvq4, kvq4)


def _out_proj_kernel(a_ref, x_ref, w_ref, o_ref):
    o_ref[...] = x_ref[...] + jnp.dot(a_ref[...].astype(_BF16), w_ref[...],
                                      preferred_element_type=_F32)


def _out_proj(att, x, w, *, tm=512):
    n, d = x.shape
    return pl.pallas_call(
        _out_proj_kernel,
        grid=(n // tm,),
        in_specs=[
            pl.BlockSpec((tm, d), lambda i: (i, 0)),
            pl.BlockSpec((tm, d), lambda i: (i, 0)),
            pl.BlockSpec((d, d), lambda i: (0, 0)),
        ],
        out_specs=pl.BlockSpec((tm, d), lambda i: (i, 0)),
        out_shape=jax.ShapeDtypeStruct((n, d), _F32),
        compiler_params=_params(1),
        name="out_proj",
    )(att, x, w)


def kernel(x, a_norm_g, conv_w1, conv_b1, conv_dw, conv_dw_b, conv_ln_g, conv_ln_b, conv_w2, conv_b2,
           kv_norm_g, w_k, w_v, b_norm_g, w_q, w_o, ffn_norm_g, ffn_w_gate, ffn_w_up, ffn_w_down,
           final_norm_g):
    batch, seq, d = x.shape
    n_a = a_norm_g.shape[0]
    n_b = b_norm_g.shape[0]
    depth = n_a + n_b
    assert d % N_HEADS == 0 and d // N_HEADS == LANES
    assert BRANCH_DILATIONS == (1, 4, 16) and seq % (max(BRANCH_DILATIONS) * Q_BLOCK * 2) == 0

    row = lambda v: v.reshape(1, -1)
    h = x.reshape(batch * seq, d)
    head_dim = d // N_HEADS
    slopes = jnp.exp2(-8.0 * jnp.arange(1, N_HEADS + 1, dtype=_F32) / N_HEADS)
    tables = jnp.asarray(_step_tables())
    kvq = None

    for layer in range(depth):
        ffn_weights = None
        if layer < n_a:
            a = layer
            w1 = _cast_stack([(conv_w1, a)])[0]
            w2 = _cast_stack([(conv_w2, a)])[0]
            u = _glu_in(h, row(a_norm_g[a]), w1, row(conv_b1[a]))
            h = _conv_out(u, h, conv_dw[a], row(conv_dw_b[a]), row(conv_ln_g[a]), row(conv_ln_b[a]),
                          w2, row(conv_b2[a]), seq=seq)
        else:
            i = layer - n_a
            assert n_b == 1, "one attention layer supported"
            w_kvq = _cast_stack([(w_k, None), (w_v, None), (w_q, i)],
                                row_scales=[kv_norm_g, kv_norm_g, b_norm_g[i]])
            kvq, *ffn_weights = _kvq(h, w_kvq, ffn_w_gate, ffn_w_up, ffn_w_down, layer,
                                     q_scale=head_dim ** -0.5 * LOG2E)
            att = _attention(kvq, slopes, tables, batch=batch, seq=seq)
            h = _out_proj(att.reshape(batch * seq, d), h, _cast_stack([(w_o, i)])[0])
        last = layer == depth - 1
        if ffn_weights is None:
            ffn_weights = (_cast_stack([(ffn_w_gate, layer), (ffn_w_up, layer)]),
                           _cast_stack([(ffn_w_down, layer)]))
        h = _ffn(h, row(ffn_norm_g[layer]), *ffn_weights, row(final_norm_g) if last else None)
    return h.reshape(batch, seq, d)
```

```python
import functools
import math

import numpy as np
import jax
import jax.numpy as jnp
from jax import lax
from jax.experimental import pallas as pl
from jax.experimental.pallas import tpu as pltpu

RMS_EPS = 1e-6
LN_EPS = 1e-5
N_HEADS = 16
CONV_WIDTH = 31
BRANCH_DILATIONS = (1, 4, 16)
DILATION_RATIO = 4
WINDOW_STEPS = 128
Q_BLOCK = 128
GROUP = 4
MASKED = 1e30
LOG2E = math.log2(math.e)
LANES = 128

_VMEM_LIMIT = 56 * 1024 * 1024
_BF16 = jnp.bfloat16
_F32 = jnp.float32


def _params(n_axes):
    return pltpu.CompilerParams(dimension_semantics=("arbitrary",) * n_axes,
                                vmem_limit_bytes=_VMEM_LIMIT)


def _sigmoid(x):
    return 1.0 / (1.0 + jnp.exp(-x))


def _rmsnorm_rows(x_ref, out_refs, g_refs, chunk=256, unroll=False):
    n = x_ref.shape[0] // chunk

    def body(i, carry):
        r = pl.multiple_of(i * chunk, chunk)
        x = x_ref[pl.ds(r, chunk), :]
        xhat = x * lax.rsqrt(jnp.mean(x * x, axis=-1, keepdims=True) + RMS_EPS)
        for o_ref, g_ref in zip(out_refs, g_refs):
            y = xhat if g_ref is None else xhat * g_ref[...]
            o_ref[pl.ds(r, chunk), :] = y.astype(o_ref.dtype)
        return carry

    lax.fori_loop(0, n, body, 0, unroll=unroll)


def _cast_kernel(*refs, n_src, scaled):
    src_refs, scale_refs, o_ref = refs[:n_src], refs[n_src:-1], refs[-1]
    which = pl.program_id(0)
    for k, src_ref in enumerate(src_refs):
        @pl.when(which == k)
        def _(k=k, src_ref=src_ref):
            w = src_ref[...]
            if scaled:
                w = w * scale_refs[k][...]
            o_ref[...] = w.astype(o_ref.dtype)


def _cast_stack(sources, row_scales=None):
    rows, cols = sources[0][0].shape[-2:]
    block_elems = (1 << 21) if len(sources) <= 2 else (1 << 20)
    rb = 1 << ((block_elems // cols).bit_length() - 1)
    while rows % rb:
        rb //= 2
    assert rb >= 16
    nb = rows // rb

    def in_spec(k, layer, width):
        def block(which, b):
            return jnp.where(which == k, b, jnp.where(which < k, 0, nb - 1))
        if layer is None:
            return pl.BlockSpec((rb, width), lambda which, b: (block(which, b), 0))
        return pl.BlockSpec((None, rb, width), lambda which, b: (layer, block(which, b), 0))

    in_specs = [in_spec(k, layer, cols) for k, (_, layer) in enumerate(sources)]
    args = [a for a, _ in sources]
    if row_scales is not None:
        in_specs += [in_spec(k, None, 1) for k in range(len(sources))]
        args += [v.reshape(rows, 1) for v in row_scales]
    return pl.pallas_call(
        functools.partial(_cast_kernel, n_src=len(sources), scaled=row_scales is not None),
        grid=(len(sources), nb),
        in_specs=in_specs,
        out_specs=pl.BlockSpec((None, rb, cols), lambda which, b: (which, b, 0)),
        out_shape=jax.ShapeDtypeStruct((len(sources), rows, cols), _BF16),
        compiler_params=_params(2),
        name="cast_bf16",
    )(*args)


def _glu_in_kernel(x_ref, g_ref, w_ref, b_ref, u_ref, xn_ref):
    tm, d = x_ref.shape
    _rmsnorm_rows(x_ref, [xn_ref], [g_ref], unroll=True)
    r = jnp.dot(xn_ref[...], w_ref[...], preferred_element_type=_F32) + b_ref[...]
    u_ref[...] = r[:, :d] * _sigmoid(r[:, d:])


def _glu_in(x, g, w1, b1, *, tm=512):
    n, d = x.shape
    return pl.pallas_call(
        _glu_in_kernel,
        grid=(n // tm,),
        in_specs=[
            pl.BlockSpec((tm, d), lambda i: (i, 0)),
            pl.BlockSpec((1, d), lambda i: (0, 0)),
            pl.BlockSpec((d, 2 * d), lambda i: (0, 0), pipeline_mode=pl.Buffered(1)),
            pl.BlockSpec((1, 2 * d), lambda i: (0, 0)),
        ],
        out_specs=pl.BlockSpec((tm, d), lambda i: (i, 0)),
        out_shape=jax.ShapeDtypeStruct((n, d), _F32),
        scratch_shapes=[pltpu.VMEM((tm, d), _BF16)],
        compiler_params=_params(1),
        name="glu_in",
    )(x, g, w1, b1)


_HALO = 32
_CONV_ROWS = 256


def _conv_out_kernel(u_ref, halo_ref, x_ref, dw_ref, dwb_ref, lng_ref, lnb_ref, w2_ref, b2_ref,
                     o_ref, buf_ref, y_ref, z_ref, *, blocks_per_seq):
    tm, d = u_ref.shape
    first = (pl.program_id(0) % blocks_per_seq) == 0
    for c in range(d // LANES):
        lanes = slice(c * LANES, (c + 1) * LANES)
        buf_ref[c, 0:_HALO, :] = jnp.where(first, 0.0, halo_ref[:, lanes])
        buf_ref[c, _HALO:, :] = u_ref[:, lanes]

    def conv_lanes(c, carry):
        lanes = pl.ds(pl.multiple_of(c * LANES, LANES), LANES)
        w = dw_ref[:, lanes]
        for r0 in range(0, tm, _CONV_ROWS):
            acc = jnp.broadcast_to(dwb_ref[:, lanes], (_CONV_ROWS, LANES))
            for k in range(CONV_WIDTH):
                shifted = buf_ref[c, pl.ds(r0 + _HALO - (CONV_WIDTH - 1) + k, _CONV_ROWS, stride=1), :]
                acc = acc + shifted * w[k:k + 1, :]
            y_ref[r0:r0 + _CONV_ROWS, lanes] = acc
        return carry

    lax.fori_loop(0, d // LANES, conv_lanes, 0)

    chunk = 256

    def norm_rows(i, carry):
        rows = pl.ds(pl.multiple_of(i * chunk, chunk), chunk)
        y = y_ref[rows, :]
        mu = jnp.mean(y, axis=-1, keepdims=True)
        yc = y - mu
        var = jnp.mean(yc * yc, axis=-1, keepdims=True)
        zn = yc * lax.rsqrt(var + LN_EPS) * lng_ref[...] + lnb_ref[...]
        z_ref[rows, :] = (zn * _sigmoid(zn)).astype(z_ref.dtype)
        return carry

    lax.fori_loop(0, tm // chunk, norm_rows, 0, unroll=True)

    o_ref[...] = (x_ref[...] + b2_ref[...]
                  + jnp.dot(z_ref[...], w2_ref[...], preferred_element_type=_F32))


def _conv_out(u, x, dw, dwb, lng, lnb, w2, b2, *, seq, tm=512):
    n, d = u.shape
    per_halo = tm // _HALO
    kern = functools.partial(_conv_out_kernel, blocks_per_seq=seq // tm)
    row = lambda i: (0, 0)
    return pl.pallas_call(
        kern,
        grid=(n // tm,),
        in_specs=[
            pl.BlockSpec((tm, d), lambda i: (i, 0)),
            pl.BlockSpec((_HALO, d), lambda i: (jnp.maximum(i * per_halo - 1, 0), 0)),
            pl.BlockSpec((tm, d), lambda i: (i, 0)),
            pl.BlockSpec((CONV_WIDTH, d), row),
            pl.BlockSpec((1, d), row),
            pl.BlockSpec((1, d), row),
            pl.BlockSpec((1, d), row),
            pl.BlockSpec((d, d), row, pipeline_mode=pl.Buffered(1)),
            pl.BlockSpec((1, d), row),
        ],
        out_specs=pl.BlockSpec((tm, d), lambda i: (i, 0)),
        out_shape=jax.ShapeDtypeStruct((n, d), _F32),
        scratch_shapes=[pltpu.VMEM((d // LANES, tm + _HALO, LANES), _F32),
                        pltpu.VMEM((tm, d), _F32),
                        pltpu.VMEM((tm, d), _BF16)],
        compiler_params=_params(1),
        name="conv_out",
    )(u, u, x, dw, dwb, lng, lnb, w2, b2)


def _ffn_kernel(x_ref, g_ref, wgu_hbm, wd_hbm, *rest, final_norm, tf):
    if final_norm:
        gf_ref, o_ref, xn_ref, wg_buf, wu_buf, wd_buf, sem = rest
    else:
        o_ref, xn_ref, wg_buf, wu_buf, wd_buf, sem = rest
    n_chunks = wd_hbm.shape[1] // tf
    i = pl.program_id(0)
    n_tiles = pl.num_programs(0)

    def chunk_copies(j, slot):
        cols = pl.ds(pl.multiple_of(j * tf, tf), tf)
        return (pltpu.make_async_copy(wgu_hbm.at[0, :, cols], wg_buf.at[slot], sem.at[0, slot]),
                pltpu.make_async_copy(wgu_hbm.at[1, :, cols], wu_buf.at[slot], sem.at[1, slot]),
                pltpu.make_async_copy(wd_hbm.at[0, cols, :], wd_buf.at[slot], sem.at[2, slot]))

    @pl.when(i == 0)
    def _():
        for copy in chunk_copies(0, 0):
            copy.start()

    _rmsnorm_rows(x_ref, [xn_ref], [g_ref])
    o_ref[...] = x_ref[...]

    def body(j, carry):
        count = i * n_chunks + j
        slot = count % 2
        is_last = jnp.logical_and(i == n_tiles - 1, j == n_chunks - 1)

        @pl.when(jnp.logical_not(is_last))
        def _():
            for copy in chunk_copies((j + 1) % n_chunks, 1 - slot):
                copy.start()

        for copy in chunk_copies(j, slot):
            copy.wait()
        xn = xn_ref[...]
        gate = jnp.dot(xn, wg_buf[slot], preferred_element_type=_F32)
        up = jnp.dot(xn, wu_buf[slot], preferred_element_type=_F32)
        act = (gate * _sigmoid(gate) * up).astype(_BF16)
        o_ref[...] += jnp.dot(act, wd_buf[slot], preferred_element_type=_F32)
        return carry

    lax.fori_loop(0, n_chunks, body, 0)

    if final_norm:
        _rmsnorm_rows(o_ref, [o_ref], [gf_ref])


def _ffn(x, g, w_gate_up, w_down, gf=None, *, tm=1024, tf=512):
    n, d = x.shape
    f = w_gate_up.shape[2]
    assert f % tf == 0
    final_norm = gf is not None
    in_specs = [
        pl.BlockSpec((tm, d), lambda i: (i, 0)),
        pl.BlockSpec((1, d), lambda i: (0, 0)),
        pl.BlockSpec(memory_space=pl.ANY),
        pl.BlockSpec(memory_space=pl.ANY),
    ]
    args = [x, g, w_gate_up, w_down]
    if final_norm:
        in_specs.append(pl.BlockSpec((1, d), lambda i: (0, 0)))
        args.append(gf)
    return pl.pallas_call(
        functools.partial(_ffn_kernel, final_norm=final_norm, tf=tf),
        grid=(n // tm,),
        in_specs=in_specs,
        out_specs=pl.BlockSpec((tm, d), lambda i: (i, 0)),
        out_shape=jax.ShapeDtypeStruct((n, d), _F32),
        scratch_shapes=[pltpu.VMEM((tm, d), _BF16),
                        pltpu.VMEM((2, d, tf), _BF16),
                        pltpu.VMEM((2, d, tf), _BF16),
                        pltpu.VMEM((2, tf, d), _BF16),
                        pltpu.SemaphoreType.DMA((3, 2))],
        compiler_params=_params(1),
        name="ffn_final" if final_norm else "ffn",
    )(*args)


def _kvq_kernel(x_ref, w_ref, gate_ref, up_ref, down_ref, o_ref, wgu_ref, wdn_ref, xn_ref,
                *, q_scale, blocks_per_proj, third):
    j = pl.program_id(1)
    step = pl.program_id(0) * pl.num_programs(1) + j

    @pl.when(j == 0)
    def _():
        _rmsnorm_rows(x_ref, [xn_ref], [None])

    @pl.when(j < 2 * blocks_per_proj)
    def _():
        o_ref[...] = jnp.dot(xn_ref[...], w_ref[...], preferred_element_type=_F32)

    @pl.when(j >= 2 * blocks_per_proj)
    def _():
        o_ref[...] = jnp.dot(xn_ref[...], w_ref[...], preferred_element_type=_F32) * q_scale

    @pl.when(step < third)
    def _():
        wgu_ref[...] = gate_ref[...].astype(wgu_ref.dtype)

    @pl.when(jnp.logical_and(step >= third, step < 2 * third))
    def _():
        wgu_ref[...] = up_ref[...].astype(wgu_ref.dtype)

    @pl.when(step >= 2 * third)
    def _():
        wdn_ref[...] = down_ref[...].astype(wdn_ref.dtype)


def _kvq(x, w_kvq, ffn_gate, ffn_up, ffn_down, layer, *, q_scale, tm=1024, tn=1024):
    n, d = x.shape
    f = ffn_gate.shape[2]
    per = d // tn
    steps_j = 3 * per
    third = (n // tm) * steps_j // 3
    gate_rows, down_rows = d // third, f // third
    assert third * 3 == (n // tm) * steps_j and gate_rows * third == d and down_rows * third == f
    assert gate_rows % 16 == 0 and down_rows % 16 == 0

    def block(i, j, k):
        return jnp.clip(i * steps_j + j - k * third, 0, third - 1)

    def gu_index(i, j):
        step = i * steps_j + j
        return jnp.minimum(step // third, 1), jnp.where(step < 2 * third, step % third, third - 1), 0

    return pl.pallas_call(
        functools.partial(_kvq_kernel, q_scale=q_scale, blocks_per_proj=per, third=third),
        grid=(n // tm, steps_j),
        in_specs=[
            pl.BlockSpec((tm, d), lambda i, j: (i, 0)),
            pl.BlockSpec((None, d, tn), lambda i, j: (j // per, 0, j % per)),
            pl.BlockSpec((None, gate_rows, f), lambda i, j: (layer, block(i, j, 0), 0)),
            pl.BlockSpec((None, gate_rows, f), lambda i, j: (layer, block(i, j, 1), 0)),
            pl.BlockSpec((None, down_rows, d), lambda i, j: (layer, block(i, j, 2), 0)),
        ],
        out_specs=[
            pl.BlockSpec((None, tm, tn), lambda i, j: (j // per, i, j % per)),
            pl.BlockSpec((None, gate_rows, f), gu_index),
            pl.BlockSpec((None, down_rows, d), lambda i, j: (0, block(i, j, 2), 0)),
        ],
        out_shape=[jax.ShapeDtypeStruct((3, n, d), _F32),
                   jax.ShapeDtypeStruct((2, d, f), _BF16),
                   jax.ShapeDtypeStruct((1, f, d), _BF16)],
        scratch_shapes=[pltpu.VMEM((tm, d), _BF16)],
        compiler_params=_params(2),
        name="kvq",
    )(x, w_kvq, ffn_gate, ffn_up, ffn_down)


def _step_tables():
    p = np.arange(Q_BLOCK)[:, None]
    c = np.arange(2 * Q_BLOCK)[None, :]
    pair = []
    for shift in (0, Q_BLOCK):
        j = shift + p - c
        valid = (j >= 0) & (j <= WINDOW_STEPS)
        pair.append(np.where(valid, j.astype(np.float32), np.float32(MASKED)))
    return np.stack(pair).astype(np.float32)


def _attention_kernel(slopes_ref, tab_ref, q_ref, k_ref, v_ref, o_ref,
                      bias_ref, qs_ref, ks_ref, vs_ref, t4_ref, s_ref, e_ref, rl_ref, ob_ref, lse_ref):
    seq = q_ref.shape[0]
    blocks_per_branch = seq // Q_BLOCK
    n_br = len(BRANCH_DILATIONS)
    n_groups = n_br * blocks_per_branch // GROUP
    assert blocks_per_branch % GROUP == 0 and n_groups % 2 == 0 and n_groups >= 4
    h = pl.program_id(1)
    neg_slope2 = -slopes_ref[h] * LOG2E
    for b, dil in enumerate(BRANCH_DILATIONS):
        for t in range(2):
            bias_ref[b, t] = tab_ref[t] * (dil * neg_slope2)

    ratio = DILATION_RATIO
    chunk = seq // max(BRANCH_DILATIONS)
    quarter = seq // ratio

    def finer(c):
        return pl.ds((c // ratio) + ratio * (c % ratio) * chunk, chunk, stride=ratio)

    def finer16(c):
        return pl.ds((c % ratio) * quarter + c // ratio, chunk, stride=ratio)

    srcs = (q_ref, k_ref, v_ref)
    dsts = (qs_ref, ks_ref, vs_ref)

    def regroup4(c, carry):
        dst = pl.ds(pl.multiple_of(c * chunk, chunk), chunk)
        for a in range(3):
            dsts[a][0, dst, :] = srcs[a][dst, :].astype(_BF16)
            by4 = srcs[a][finer(c), :]
            t4_ref[a, dst, :] = by4
            dsts[a][1, dst, :] = by4.astype(_BF16)
        return carry

    lax.fori_loop(0, seq // chunk, regroup4, 0, unroll=2)

    def regroup16(c, carry):
        dst = pl.ds(pl.multiple_of(c * chunk, chunk), chunk)
        for a in range(3):
            dsts[a][2, dst, :] = t4_ref[a, finer16(c), :].astype(_BF16)
        return carry

    lax.fori_loop(0, seq // chunk, regroup16, 0, unroll=2)

    def locate(g, u):
        t = g * GROUP + u
        b = t // blocks_per_branch
        jb = t % blocks_per_branch
        per_class = lax.shift_right_logical(blocks_per_branch, 2 * b)
        is_first = (jb & (per_class - 1)) == 0
        q0 = pl.multiple_of(jb * Q_BLOCK, Q_BLOCK)
        k0 = pl.multiple_of(jnp.where(is_first, q0, q0 - Q_BLOCK), Q_BLOCK)
        return b, q0, k0, jnp.where(is_first, 0, 1)

    def scores(g, slot):
        for u in range(GROUP):
            b, q0, k0, _ = locate(g, u)
            q = qs_ref[b, pl.ds(q0, Q_BLOCK), :]
            k = ks_ref[b, pl.ds(k0, 2 * Q_BLOCK), :]
            s_ref[slot, u] = lax.dot_general(q, k, (((1,), (1,)), ((), ())),
                                             preferred_element_type=_F32)

    def softmax(g, slot):
        for u in range(GROUP):
            b, q0, _, tab = locate(g, u)
            s = s_ref[slot, u] + bias_ref[b, tab]
            m = jnp.max(s, axis=-1, keepdims=True)
            e = jnp.exp2(s - m)
            l = jnp.sum(e, axis=-1, keepdims=True)
            e_ref[slot, u] = e.astype(_BF16)
            rl_ref[slot, u] = jnp.broadcast_to(1.0 / l, (Q_BLOCK, LANES))
            lse_ref[b, pl.ds(q0, Q_BLOCK), :] = jnp.broadcast_to(m + jnp.log2(l), (Q_BLOCK, LANES))

    def values(g, slot):
        for u in range(GROUP):
            b, q0, k0, _ = locate(g, u)
            v = vs_ref[b, pl.ds(k0, 2 * Q_BLOCK), :]
            acc = jnp.dot(e_ref[slot, u], v, preferred_element_type=_F32)
            ob_ref[b, pl.ds(q0, Q_BLOCK), :] = acc * rl_ref[slot, u]

    zero = jnp.int32(0)
    scores(zero, 0)
    softmax(zero, 0)
    scores(zero + 1, 1)

    def trip(i, carry):
        g = 2 * i
        values(g - 2, 0)
        softmax(g - 1, 1)
        scores(g, 0)
        values(g - 1, 1)
        softmax(g, 0)
        scores(g + 1, 1)
        return carry

    lax.fori_loop(1, n_groups // 2, trip, 0, unroll=True)
    values(zero + (n_groups - 2), 0)
    softmax(zero + (n_groups - 1), 1)
    values(zero + (n_groups - 1), 1)

    def coarsen(c, carry):
        src = pl.ds(pl.multiple_of(c * chunk, chunk), chunk)
        t4_ref[0, finer16(c), :] = ob_ref[2, src, :]
        t4_ref[1, finer16(c), :] = lse_ref[2, src, :]
        return carry

    lax.fori_loop(0, seq // chunk, coarsen, 0, unroll=2)

    rows_per = 1024

    def merge(c, carry):
        by4 = pl.ds(pl.multiple_of(c * rows_per, rows_per), rows_per)
        per_class = quarter // rows_per
        natural = pl.ds((c // per_class) + ratio * (c % per_class) * rows_per, rows_per, stride=ratio)
        ls = [lse_ref[0, natural, :], lse_ref[1, by4, :], t4_ref[1, by4, :]]
        os = [ob_ref[0, natural, :], ob_ref[1, by4, :], t4_ref[0, by4, :]]
        l_max = functools.reduce(jnp.maximum, ls)
        num = 0.0
        den = 0.0
        for b in range(n_br):
            w = jnp.exp2(ls[b] - l_max)
            num = num + w * os[b]
            den = den + w
        o_ref[natural, :] = num / den
        return carry

    lax.fori_loop(0, seq // rows_per, merge, 0)


def _attention(kvq, slopes, tables, *, batch, seq):
    _, n, d = kvq.shape
    e = d // N_HEADS
    n_br = len(BRANCH_DILATIONS)
    kvq4 = kvq.reshape(3, batch, seq, d)

    def qkv_spec(which):
        return pl.BlockSpec((None, None, seq, e), lambda b, h: (which, b, 0, h))

    return pl.pallas_call(
        _attention_kernel,
        grid=(batch, N_HEADS),
        in_specs=[
            pl.BlockSpec(memory_space=pltpu.SMEM),
            pl.BlockSpec(tables.shape, lambda b, h: (0, 0, 0)),
            qkv_spec(2), qkv_spec(0), qkv_spec(1),
        ],
        out_specs=pl.BlockSpec((None, seq, e), lambda b, h: (b, 0, h)),
        out_shape=jax.ShapeDtypeStruct((batch, seq, d), _F32),
        scratch_shapes=[
            pltpu.VMEM((n_br, 2, Q_BLOCK, 2 * Q_BLOCK), _F32),
            pltpu.VMEM((n_br, seq, e), _BF16),
            pltpu.VMEM((n_br, seq, e), _BF16),
            pltpu.VMEM((n_br, seq, e), _BF16),
            pltpu.VMEM((3, seq, e), _F32),
            pltpu.VMEM((2, GROUP, Q_BLOCK, 2 * Q_BLOCK), _F32),
            pltpu.VMEM((2, GROUP, Q_BLOCK, 2 * Q_BLOCK), _BF16),
            pltpu.VMEM((2, GROUP, Q_BLOCK, LANES), _F32),
            pltpu.VMEM((n_br, seq, e), _F32),
            pltpu.VMEM((n_br, seq, LANES), _F32),
        ],
        compiler_params=_params(2),
        name="dilated_attention",
    )(slopes, tables, kvq4, kvq4, kvq4)


def _out_proj_kernel(a_ref, x_ref, w_ref, o_ref):
    o_ref[...] = x_ref[...] + jnp.dot(a_ref[...].astype(_BF16), w_ref[...],
                                      preferred_element_type=_F32)


def _out_proj(att, x, w, *, tm=512):
    n, d = x.shape
    return pl.pallas_call(
        _out_proj_kernel,
        grid=(n // tm,),
        in_specs=[
            pl.BlockSpec((tm, d), lambda i: (i, 0)),
            pl.BlockSpec((tm, d), lambda i: (i, 0)),
            pl.BlockSpec((d, d), lambda i: (0, 0)),
        ],
        out_specs=pl.BlockSpec((tm, d), lambda i: (i, 0)),
        out_shape=jax.ShapeDtypeStruct((n, d), _F32),
        compiler_params=_params(1),
        name="out_proj",
    )(att, x, w)


def kernel(x, a_norm_g, conv_w1, conv_b1, conv_dw, conv_dw_b, conv_ln_g, conv_ln_b, conv_w2, conv_b2,
           kv_norm_g, w_k, w_v, b_norm_g, w_q, w_o, ffn_norm_g, ffn_w_gate, ffn_w_up, ffn_w_down,
           final_norm_g):
    batch, seq, d = x.shape
    n_a = a_norm_g.shape[0]
    n_b = b_norm_g.shape[0]
    depth = n_a + n_b
    assert d % N_HEADS == 0 and d // N_HEADS == LANES
    assert BRANCH_DILATIONS == (1, 4, 16) and seq % (max(BRANCH_DILATIONS) * Q_BLOCK * 2) == 0

    row = lambda v: v.reshape(1, -1)
    h = x.reshape(batch * seq, d)
    head_dim = d // N_HEADS
    slopes = jnp.exp2(-8.0 * jnp.arange(1, N_HEADS + 1, dtype=_F32) / N_HEADS)
    tables = jnp.asarray(_step_tables())
    kvq = None

    for layer in range(depth):
        ffn_weights = None
        if layer < n_a:
            a = layer
            w1 = _cast_stack([(conv_w1, a)])[0]
            w2 = _cast_stack([(conv_w2, a)])[0]
            u = _glu_in(h, row(a_norm_g[a]), w1, row(conv_b1[a]))
            h = _conv_out(u, h, conv_dw[a], row(conv_dw_b[a]), row(conv_ln_g[a]), row(conv_ln_b[a]),
                          w2, row(conv_b2[a]), seq=seq)
        else:
            i = layer - n_a
            assert n_b == 1, "one attention layer supported"
            w_kvq = _cast_stack([(w_k, None), (w_v, None), (w_q, i)],
                                row_scales=[kv_norm_g, kv_norm_g, b_norm_g[i]])
            kvq, *ffn_weights = _kvq(h, w_kvq, ffn_w_gate, ffn_w_up, ffn_w_down, layer,
                                     q_scale=head_dim ** -0.5 * LOG2E)
            att = _attention(kvq, slopes, tables, batch=batch, seq=seq)
            h = _out_proj(att.reshape(batch * seq, d), h, _cast_stack([(w_o, i)])[0])
        last = layer == depth - 1
        if ffn_weights is None:
            ffn_weights = (_cast_stack([(ffn_w_gate, layer), (ffn_w_up, layer)]),
                           _cast_stack([(ffn_w_down, layer)]))
        h = _ffn(h, row(ffn_norm_g[layer]), *ffn_weights, row(final_norm_g) if last else None)
    return h.reshape(batch, seq, d)
```

```python
import functools
import math

import numpy as np
import jax
import jax.numpy as jnp
from jax import lax
from jax.experimental import pallas as pl
from jax.experimental.pallas import tpu as pltpu

RMS_EPS = 1e-6
LN_EPS = 1e-5
N_HEADS = 16
CONV_WIDTH = 31
BRANCH_DILATIONS = (1, 4, 16)
DILATION_RATIO = 4
WINDOW_STEPS = 128
Q_BLOCK = 128
GROUP = 4
MASKED = 1e30
LOG2E = math.log2(math.e)
LANES = 128

_VMEM_LIMIT = 56 * 1024 * 1024
_BF16 = jnp.bfloat16
_F32 = jnp.float32


def _params(n_axes):
    return pltpu.CompilerParams(dimension_semantics=("arbitrary",) * n_axes,
                                vmem_limit_bytes=_VMEM_LIMIT)


def _sigmoid(x):
    return 1.0 / (1.0 + jnp.exp(-x))


def _rmsnorm_rows(x_ref, out_refs, g_refs, chunk=256, unroll=False):
    n = x_ref.shape[0] // chunk

    def body(i, carry):
        r = pl.multiple_of(i * chunk, chunk)
        x = x_ref[pl.ds(r, chunk), :]
        xhat = x * lax.rsqrt(jnp.mean(x * x, axis=-1, keepdims=True) + RMS_EPS)
        for o_ref, g_ref in zip(out_refs, g_refs):
            y = xhat if g_ref is None else xhat * g_ref[...]
            o_ref[pl.ds(r, chunk), :] = y.astype(o_ref.dtype)
        return carry

    lax.fori_loop(0, n, body, 0, unroll=unroll)


def _cast_kernel(*refs, n_src, scaled):
    src_refs, scale_refs, o_ref = refs[:n_src], refs[n_src:-1], refs[-1]
    which = pl.program_id(0)
    for k, src_ref in enumerate(src_refs):
        @pl.when(which == k)
        def _(k=k, src_ref=src_ref):
            w = src_ref[...]
            if scaled:
                w = w * scale_refs[k][...]
            o_ref[...] = w.astype(o_ref.dtype)


def _cast_stack(sources, row_scales=None):
    rows, cols = sources[0][0].shape[-2:]
    block_elems = (1 << 21) if len(sources) <= 2 else (1 << 20)
    rb = 1 << ((block_elems // cols).bit_length() - 1)
    while rows % rb:
        rb //= 2
    assert rb >= 16
    nb = rows // rb

    def in_spec(k, layer, width):
        def block(which, b):
            return jnp.where(which == k, b, jnp.where(which < k, 0, nb - 1))
        if layer is None:
            return pl.BlockSpec((rb, width), lambda which, b: (block(which, b), 0))
        return pl.BlockSpec((None, rb, width), lambda which, b: (layer, block(which, b), 0))

    in_specs = [in_spec(k, layer, cols) for k, (_, layer) in enumerate(sources)]
    args = [a for a, _ in sources]
    if row_scales is not None:
        in_specs += [in_spec(k, None, 1) for k in range(len(sources))]
        args += [v.reshape(rows, 1) for v in row_scales]
    return pl.pallas_call(
        functools.partial(_cast_kernel, n_src=len(sources), scaled=row_scales is not None),
        grid=(len(sources), nb),
        in_specs=in_specs,
        out_specs=pl.BlockSpec((None, rb, cols), lambda which, b: (which, b, 0)),
        out_shape=jax.ShapeDtypeStruct((len(sources), rows, cols), _BF16),
        compiler_params=_params(2),
        name="cast_bf16",
    )(*args)


def _glu_in_kernel(x_ref, g_ref, w_ref, b_ref, u_ref, xn_ref):
    tm, d = x_ref.shape
    _rmsnorm_rows(x_ref, [xn_ref], [g_ref], unroll=True)
    r = jnp.dot(xn_ref[...], w_ref[...], preferred_element_type=_F32) + b_ref[...]
    u_ref[...] = r[:, :d] * _sigmoid(r[:, d:])


def _glu_in(x, g, w1, b1, *, tm=512):
    n, d = x.shape
    return pl.pallas_call(
        _glu_in_kernel,
        grid=(n // tm,),
        in_specs=[
            pl.BlockSpec((tm, d), lambda i: (i, 0)),
            pl.BlockSpec((1, d), lambda i: (0, 0)),
            pl.BlockSpec((d, 2 * d), lambda i: (0, 0), pipeline_mode=pl.Buffered(1)),
            pl.BlockSpec((1, 2 * d), lambda i: (0, 0)),
        ],
        out_specs=pl.BlockSpec((tm, d), lambda i: (i, 0)),
        out_shape=jax.ShapeDtypeStruct((n, d), _F32),
        scratch_shapes=[pltpu.VMEM((tm, d), _BF16)],
        compiler_params=_params(1),
        name="glu_in",
    )(x, g, w1, b1)


_HALO = 32
_CONV_ROWS = 256


def _conv_out_kernel(u_ref, halo_ref, x_ref, dw_ref, dwb_ref, lng_ref, lnb_ref, w2_ref, b2_ref,
                     o_ref, buf_ref, y_ref, z_ref, *, blocks_per_seq):
    tm, d = u_ref.shape
    first = (pl.program_id(0) % blocks_per_seq) == 0
    for c in range(d // LANES):
        lanes = slice(c * LANES, (c + 1) * LANES)
        buf_ref[c, 0:_HALO, :] = jnp.where(first, 0.0, halo_ref[:, lanes])
        buf_ref[c, _HALO:, :] = u_ref[:, lanes]

    def conv_lanes(c, carry):
        lanes = pl.ds(pl.multiple_of(c * LANES, LANES), LANES)
        w = dw_ref[:, lanes]
        for r0 in range(0, tm, _CONV_ROWS):
            acc = jnp.broadcast_to(dwb_ref[:, lanes], (_CONV_ROWS, LANES))
            for k in range(CONV_WIDTH):
                shifted = buf_ref[c, pl.ds(r0 + _HALO - (CONV_WIDTH - 1) + k, _CONV_ROWS, stride=1), :]
                acc = acc + shifted * w[k:k + 1, :]
            y_ref[r0:r0 + _CONV_ROWS, lanes] = acc
        return carry

    lax.fori_loop(0, d // LANES, conv_lanes, 0)

    chunk = 256

    def norm_rows(i, carry):
        rows = pl.ds(pl.multiple_of(i * chunk, chunk), chunk)
        y = y_ref[rows, :]
        mu = jnp.mean(y, axis=-1, keepdims=True)
        yc = y - mu
        var = jnp.mean(yc * yc, axis=-1, keepdims=True)
        zn = yc * lax.rsqrt(var + LN_EPS) * lng_ref[...] + lnb_ref[...]
        z_ref[rows, :] = (zn * _sigmoid(zn)).astype(z_ref.dtype)
        return carry

    lax.fori_loop(0, tm // chunk, norm_rows, 0, unroll=True)

    o_ref[...] = (x_ref[...] + b2_ref[...]
                  + jnp.dot(z_ref[...], w2_ref[...], preferred_element_type=_F32))


def _conv_out(u, x, dw, dwb, lng, lnb, w2, b2, *, seq, tm=512):
    n, d = u.shape
    per_halo = tm // _HALO
    kern = functools.partial(_conv_out_kernel, blocks_per_seq=seq // tm)
    row = lambda i: (0, 0)
    return pl.pallas_call(
        kern,
        grid=(n // tm,),
        in_specs=[
            pl.BlockSpec((tm, d), lambda i: (i, 0)),
            pl.BlockSpec((_HALO, d), lambda i: (jnp.maximum(i * per_halo - 1, 0), 0)),
            pl.BlockSpec((tm, d), lambda i: (i, 0)),
            pl.BlockSpec((CONV_WIDTH, d), row),
            pl.BlockSpec((1, d), row),
            pl.BlockSpec((1, d), row),
            pl.BlockSpec((1, d), row),
            pl.BlockSpec((d, d), row, pipeline_mode=pl.Buffered(1)),
            pl.BlockSpec((1, d), row),
        ],
        out_specs=pl.BlockSpec((tm, d), lambda i: (i, 0)),
        out_shape=jax.ShapeDtypeStruct((n, d), _F32),
        scratch_shapes=[pltpu.VMEM((d // LANES, tm + _HALO, LANES), _F32),
                        pltpu.VMEM((tm, d), _F32),
                        pltpu.VMEM((tm, d), _BF16)],
        compiler_params=_params(1),
        name="conv_out",
    )(u, u, x, dw, dwb, lng, lnb, w2, b2)


def _ffn_kernel(x_ref, g_ref, wgu_hbm, wd_hbm, *rest, final_norm, tf):
    if final_norm:
        gf_ref, o_ref, xn_ref, wg_buf, wu_buf, wd_buf, sem = rest
    else:
        o_ref, xn_ref, wg_buf, wu_buf, wd_buf, sem = rest
    n_chunks = wd_hbm.shape[1] // tf
    i = pl.program_id(0)
    n_tiles = pl.num_programs(0)

    def chunk_copies(j, slot):
        cols = pl.ds(pl.multiple_of(j * tf, tf), tf)
        return (pltpu.make_async_copy(wgu_hbm.at[0, :, cols], wg_buf.at[slot], sem.at[0, slot]),
                pltpu.make_async_copy(wgu_hbm.at[1, :, cols], wu_buf.at[slot], sem.at[1, slot]),
                pltpu.make_async_copy(wd_hbm.at[0, cols, :], wd_buf.at[slot], sem.at[2, slot]))

    @pl.when(i == 0)
    def _():
        for copy in chunk_copies(0, 0):
            copy.start()

    _rmsnorm_rows(x_ref, [xn_ref], [g_ref])
    o_ref[...] = x_ref[...]

    def body(j, carry):
        count = i * n_chunks + j
        slot = count % 2
        for copy in chunk_copies((j + 1) % n_chunks, 1 - slot):
            copy.start()

        for copy in chunk_copies(j, slot):
            copy.wait()
        xn = xn_ref[...]
        gate = jnp.dot(xn, wg_buf[slot], preferred_element_type=_F32)
        up = jnp.dot(xn, wu_buf[slot], preferred_element_type=_F32)
        act = (gate * _sigmoid(gate) * up).astype(_BF16)
        o_ref[...] += jnp.dot(act, wd_buf[slot], preferred_element_type=_F32)
        return carry

    lax.fori_loop(0, n_chunks, body, 0)

    @pl.when(i == n_tiles - 1)
    def _():
        for copy in chunk_copies(0, (n_tiles * n_chunks) % 2):
            copy.wait()

    if final_norm:
        _rmsnorm_rows(o_ref, [o_ref], [gf_ref])


def _ffn(x, g, w_gate_up, w_down, gf=None, *, tm=1024, tf=512):
    n, d = x.shape
    f = w_gate_up.shape[2]
    assert f % tf == 0
    final_norm = gf is not None
    in_specs = [
        pl.BlockSpec((tm, d), lambda i: (i, 0)),
        pl.BlockSpec((1, d), lambda i: (0, 0)),
        pl.BlockSpec(memory_space=pl.ANY),
        pl.BlockSpec(memory_space=pl.ANY),
    ]
    args = [x, g, w_gate_up, w_down]
    if final_norm:
        in_specs.append(pl.BlockSpec((1, d), lambda i: (0, 0)))
        args.append(gf)
    return pl.pallas_call(
        functools.partial(_ffn_kernel, final_norm=final_norm, tf=tf),
        grid=(n // tm,),
        in_specs=in_specs,
        out_specs=pl.BlockSpec((tm, d), lambda i: (i, 0)),
        out_shape=jax.ShapeDtypeStruct((n, d), _F32),
        scratch_shapes=[pltpu.VMEM((tm, d), _BF16),
                        pltpu.VMEM((2, d, tf), _BF16),
                        pltpu.VMEM((2, d, tf), _BF16),
                        pltpu.VMEM((2, tf, d), _BF16),
                        pltpu.SemaphoreType.DMA((3, 2))],
        compiler_params=_params(1),
        name="ffn_final" if final_norm else "ffn",
    )(*args)


def _kvq_kernel(x_ref, w_ref, gate_ref, up_ref, down_ref, o_ref, wgu_ref, wdn_ref, xn_ref,
                *, q_scale, blocks_per_proj, third):
    j = pl.program_id(1)
    step = pl.program_id(0) * pl.num_programs(1) + j

    @pl.when(j == 0)
    def _():
        _rmsnorm_rows(x_ref, [xn_ref], [None])

    @pl.when(j < 2 * blocks_per_proj)
    def _():
        o_ref[...] = jnp.dot(xn_ref[...], w_ref[...], preferred_element_type=_F32)

    @pl.when(j >= 2 * blocks_per_proj)
    def _():
        o_ref[...] = jnp.dot(xn_ref[...], w_ref[...], preferred_element_type=_F32) * q_scale

    @pl.when(step < third)
    def _():
        wgu_ref[...] = gate_ref[...].astype(wgu_ref.dtype)

    @pl.when(jnp.logical_and(step >= third, step < 2 * third))
    def _():
        wgu_ref[...] = up_ref[...].astype(wgu_ref.dtype)

    @pl.when(step >= 2 * third)
    def _():
        wdn_ref[...] = down_ref[...].astype(wdn_ref.dtype)


def _kvq(x, w_kvq, ffn_gate, ffn_up, ffn_down, layer, *, q_scale, tm=1024, tn=1024):
    n, d = x.shape
    f = ffn_gate.shape[2]
    per = d // tn
    steps_j = 3 * per
    third = (n // tm) * steps_j // 3
    gate_rows, down_rows = d // third, f // third
    assert third * 3 == (n // tm) * steps_j and gate_rows * third == d and down_rows * third == f
    assert gate_rows % 16 == 0 and down_rows % 16 == 0

    def block(i, j, k):
        return jnp.clip(i * steps_j + j - k * third, 0, third - 1)

    def gu_index(i, j):
        step = i * steps_j + j
        return jnp.minimum(step // third, 1), jnp.where(step < 2 * third, step % third, third - 1), 0

    return pl.pallas_call(
        functools.partial(_kvq_kernel, q_scale=q_scale, blocks_per_proj=per, third=third),
        grid=(n // tm, steps_j),
        in_specs=[
            pl.BlockSpec((tm, d), lambda i, j: (i, 0)),
            pl.BlockSpec((None, d, tn), lambda i, j: (j // per, 0, j % per)),
            pl.BlockSpec((None, gate_rows, f), lambda i, j: (layer, block(i, j, 0), 0)),
            pl.BlockSpec((None, gate_rows, f), lambda i, j: (layer, block(i, j, 1), 0)),
            pl.BlockSpec((None, down_rows, d), lambda i, j: (layer, block(i, j, 2), 0)),
        ],
        out_specs=[
            pl.BlockSpec((None, tm, tn), lambda i, j: (j // per, i, j % per)),
            pl.BlockSpec((None, gate_rows, f), gu_index),
            pl.BlockSpec((None, down_rows, d), lambda i, j: (0, block(i, j, 2), 0)),
        ],
        out_shape=[jax.ShapeDtypeStruct((3, n, d), _F32),
                   jax.ShapeDtypeStruct((2, d, f), _BF16),
                   jax.ShapeDtypeStruct((1, f, d), _BF16)],
        scratch_shapes=[pltpu.VMEM((tm, d), _BF16)],
        compiler_params=_params(2),
        name="kvq",
    )(x, w_kvq, ffn_gate, ffn_up, ffn_down)


def _step_tables():
    p = np.arange(Q_BLOCK)[:, None]
    c = np.arange(2 * Q_BLOCK)[None, :]
    pair = []
    for shift in (0, Q_BLOCK):
        j = shift + p - c
        valid = (j >= 0) & (j <= WINDOW_STEPS)
        pair.append(np.where(valid, j.astype(np.float32), np.float32(MASKED)))
    return np.stack(pair).astype(np.float32)


def _attention_kernel(slopes_ref, tab_ref, q_ref, k_ref, v_ref, o_ref,
                      bias_ref, qs_ref, ks_ref, vs_ref, t4_ref, s_ref, e_ref, rl_ref, ob_ref, lse_ref):
    seq = q_ref.shape[0]
    blocks_per_branch = seq // Q_BLOCK
    n_br = len(BRANCH_DILATIONS)
    n_groups = n_br * blocks_per_branch // GROUP
    assert blocks_per_branch % GROUP == 0 and n_groups % 2 == 0 and n_groups >= 4
    h = pl.program_id(1)
    neg_slope2 = -slopes_ref[h] * LOG2E
    for b, dil in enumerate(BRANCH_DILATIONS):
        for t in range(2):
            bias_ref[b, t] = tab_ref[t] * (dil * neg_slope2)

    ratio = DILATION_RATIO
    chunk = seq // max(BRANCH_DILATIONS)
    quarter = seq // ratio

    def finer(c):
        return pl.ds((c // ratio) + ratio * (c % ratio) * chunk, chunk, stride=ratio)

    def finer16(c):
        return pl.ds((c % ratio) * quarter + c // ratio, chunk, stride=ratio)

    srcs = (q_ref, k_ref, v_ref)
    dsts = (qs_ref, ks_ref, vs_ref)

    def regroup4(c, carry):
        dst = pl.ds(pl.multiple_of(c * chunk, chunk), chunk)
        for a in range(3):
            dsts[a][0, dst, :] = srcs[a][dst, :].astype(_BF16)
            by4 = srcs[a][finer(c), :]
            t4_ref[a, dst, :] = by4
            dsts[a][1, dst, :] = by4.astype(_BF16)
        return carry

    lax.fori_loop(0, seq // chunk, regroup4, 0, unroll=2)

    def regroup16(c, carry):
        dst = pl.ds(pl.multiple_of(c * chunk, chunk), chunk)
        for a in range(3):
            dsts[a][2, dst, :] = t4_ref[a, finer16(c), :].astype(_BF16)
        return carry

    lax.fori_loop(0, seq // chunk, regroup16, 0, unroll=2)

    def locate(g, u):
        t = g * GROUP + u
        b = t // blocks_per_branch
        jb = t % blocks_per_branch
        per_class = lax.shift_right_logical(blocks_per_branch, 2 * b)
        is_first = (jb & (per_class - 1)) == 0
        q0 = pl.multiple_of(jb * Q_BLOCK, Q_BLOCK)
        k0 = pl.multiple_of(jnp.where(is_first, q0, q0 - Q_BLOCK), Q_BLOCK)
        return b, q0, k0, jnp.where(is_first, 0, 1)

    def scores(g, slot):
        for u in range(GROUP):
            b, q0, k0, _ = locate(g, u)
            q = qs_ref[b, pl.ds(q0, Q_BLOCK), :]
            k = ks_ref[b, pl.ds(k0, 2 * Q_BLOCK), :]
            s_ref[slot, u] = lax.dot_general(q, k, (((1,), (1,)), ((), ())),
                                             preferred_element_type=_F32)

    def softmax(g, slot):
        for u in range(GROUP):
            b, q0, _, tab = locate(g, u)
            s = s_ref[slot, u] + bias_ref[b, tab]
            m = jnp.max(s, axis=-1, keepdims=True)
            e = jnp.exp2(s - m)
            l = jnp.sum(e, axis=-1, keepdims=True)
            e_ref[slot, u] = e.astype(_BF16)
            rl_ref[slot, u] = jnp.broadcast_to(1.0 / l, (Q_BLOCK, LANES))
            lse_ref[b, pl.ds(q0, Q_BLOCK), :] = jnp.broadcast_to(m + jnp.log2(l), (Q_BLOCK, LANES))

    def values(g, slot):
        for u in range(GROUP):
            b, q0, k0, _ = locate(g, u)
            v = vs_ref[b, pl.ds(k0, 2 * Q_BLOCK), :]
            acc = jnp.dot(e_ref[slot, u], v, preferred_element_type=_F32)
            ob_ref[b, pl.ds(q0, Q_BLOCK), :] = acc * rl_ref[slot, u]

    zero = jnp.int32(0)
    scores(zero, 0)
    softmax(zero, 0)
    scores(zero + 1, 1)

    def trip(i, carry):
        g = 2 * i
        values(g - 2, 0)
        softmax(g - 1, 1)
        scores(g, 0)
        values(g - 1, 1)
        softmax(g, 0)
        scores(g + 1, 1)
        return carry

    lax.fori_loop(1, n_groups // 2, trip, 0, unroll=True)
    values(zero + (n_groups - 2), 0)
    softmax(zero + (n_groups - 1), 1)
    values(zero + (n_groups - 1), 1)

    def coarsen(c, carry):
        src = pl.ds(pl.multiple_of(c * chunk, chunk), chunk)
        t4_ref[0, finer16(c), :] = ob_ref[2, src, :]
        t4_ref[1, finer16(c), :] = lse_ref[2, src, :]
        return carry

    lax.fori_loop(0, seq // chunk, coarsen, 0, unroll=2)

    rows_per = 1024

    def merge(c, carry):
        by4 = pl.ds(pl.multiple_of(c * rows_per, rows_per), rows_per)
        per_class = quarter // rows_per
        natural = pl.ds((c // per_class) + ratio * (c % per_class) * rows_per, rows_per, stride=ratio)
        ls = [lse_ref[0, natural, :], lse_ref[1, by4, :], t4_ref[1, by4, :]]
        os = [ob_ref[0, natural, :], ob_ref[1, by4, :], t4_ref[0, by4, :]]
        l_max = functools.reduce(jnp.maximum, ls)
        num = 0.0
        den = 0.0
        for b in range(n_br):
            w = jnp.exp2(ls[b] - l_max)
            num = num + w * os[b]
            den = den + w
        o_ref[natural, :] = num / den
        return carry

    lax.fori_loop(0, seq // rows_per, merge, 0)


def _attention(kvq, slopes, tables, *, batch, seq):
    _, n, d = kvq.shape
    e = d // N_HEADS
    n_br = len(BRANCH_DILATIONS)
    kvq4 = kvq.reshape(3, batch, seq, d)

    def qkv_spec(which):
        return pl.BlockSpec((None, None, seq, e), lambda b, h: (which, b, 0, h))

    return pl.pallas_call(
        _attention_kernel,
        grid=(batch, N_HEADS),
        in_specs=[
            pl.BlockSpec(memory_space=pltpu.SMEM),
            pl.BlockSpec(tables.shape, lambda b, h: (0, 0, 0)),
            qkv_spec(2), qkv_spec(0), qkv_spec(1),
        ],
        out_specs=pl.BlockSpec((None, seq, e), lambda b, h: (b, 0, h)),
        out_shape=jax.ShapeDtypeStruct((batch, seq, d), _F32),
        scratch_shapes=[
            pltpu.VMEM((n_br, 2, Q_BLOCK, 2 * Q_BLOCK), _F32),
            pltpu.VMEM((n_br, seq, e), _BF16),
            pltpu.VMEM((n_br, seq, e), _BF16),
            pltpu.VMEM((n_br, seq, e), _BF16),
            pltpu.VMEM((3, seq, e), _F32),
            pltpu.VMEM((2, GROUP, Q_BLOCK, 2 * Q_BLOCK), _F32),
            pltpu.VMEM((2, GROUP, Q_BLOCK, 2 * Q_BLOCK), _BF16),
            pltpu.VMEM((2, GROUP, Q_BLOCK, LANES), _F32),
            pltpu.VMEM((n_br, seq, e), _F32),
            pltpu.VMEM((n_br, seq, LANES), _F32),
        ],
        compiler_params=_params(2),
        name="dilated_attention",
    )(slopes, tables, kvq4, kvq4, kvq4)


def _out_proj_kernel(a_ref, x_ref, w_ref, o_ref):
    o_ref[...] = x_ref[...] + jnp.dot(a_ref[...].astype(_BF16), w_ref[...],
                                      preferred_element_type=_F32)


def _out_proj(att, x, w, *, tm=512):
    n, d = x.shape
    return pl.pallas_call(
        _out_proj_kernel,
        grid=(n // tm,),
        in_specs=[
            pl.BlockSpec((tm, d), lambda i: (i, 0)),
            pl.BlockSpec((tm, d), lambda i: (i, 0)),
            pl.BlockSpec((d, d), lambda i: (0, 0)),
        ],
        out_specs=pl.BlockSpec((tm, d), lambda i: (i, 0)),
        out_shape=jax.ShapeDtypeStruct((n, d), _F32),
        compiler_params=_params(1),
        name="out_proj",
    )(att, x, w)


def kernel(x, a_norm_g, conv_w1, conv_b1, conv_dw, conv_dw_b, conv_ln_g, conv_ln_b, conv_w2, conv_b2,
           kv_norm_g, w_k, w_v, b_norm_g, w_q, w_o, ffn_norm_g, ffn_w_gate, ffn_w_up, ffn_w_down,
           final_norm_g):
    batch, seq, d = x.shape
    n_a = a_norm_g.shape[0]
    n_b = b_norm_g.shape[0]
    depth = n_a + n_b
    assert d % N_HEADS == 0 and d // N_HEADS == LANES
    assert BRANCH_DILATIONS == (1, 4, 16) and seq % (max(BRANCH_DILATIONS) * Q_BLOCK * 2) == 0

    row = lambda v: v.reshape(1, -1)
    h = x.reshape(batch * seq, d)
    head_dim = d // N_HEADS
    slopes = jnp.exp2(-8.0 * jnp.arange(1, N_HEADS + 1, dtype=_F32) / N_HEADS)
    tables = jnp.asarray(_step_tables())
    kvq = None

    for layer in range(depth):
        ffn_weights = None
        if layer < n_a:
            a = layer
            w1 = _cast_stack([(conv_w1, a)])[0]
            w2 = _cast_stack([(conv_w2, a)])[0]
            u = _glu_in(h, row(a_norm_g[a]), w1, row(conv_b1[a]))
            h = _conv_out(u, h, conv_dw[a], row(conv_dw_b[a]), row(conv_ln_g[a]), row(conv_ln_b[a]),
                          w2, row(conv_b2[a]), seq=seq)
        else:
            i = layer - n_a
            assert n_b == 1, "one attention layer supported"
            w_kvq = _cast_stack([(w_k, None), (w_v, None), (w_q, i)],
                                row_scales=[kv_norm_g, kv_norm_g, b_norm_g[i]])
            kvq, *ffn_weights = _kvq(h, w_kvq, ffn_w_gate, ffn_w_up, ffn_w_down, layer,
                                     q_scale=head_dim ** -0.5 * LOG2E)
            att = _attention(kvq, slopes, tables, batch=batch, seq=seq)
            h = _out_proj(att.reshape(batch * seq, d), h, _cast_stack([(w_o, i)])[0])
        last = layer == depth - 1
        if ffn_weights is None:
            ffn_weights = (_cast_stack([(ffn_w_gate, layer), (ffn_w_up, layer)]),
                           _cast_stack([(ffn_w_down, layer)]))
        h = _ffn(h, row(ffn_norm_g[layer]), *ffn_weights, row(final_norm_g) if last else None)
    return h.reshape(batch, seq, d)
```
